```python
import math
import jax, jax.numpy as jnp
from jax import lax
import numpy as np

D_MODEL = 1024
BATCH = 8
SEQ = 2048
DEPTH = 2
DEC_BATCH = 128
DEC_SEQ = 4
PAST_LEN = 2048
PAGE_SIZE = 128

CONV_W = 4
DN_HEADS = 8
DN_DK = 128
DN_DV = 128
DN_WIDTH = DN_HEADS * DN_DK
DN_CHUNK = 64
DIL_GROUPS = ((128, 1), (512, 4), (2048, 16))
DIL_HPG = 4
DIL_HEAD_DIM = 128
DIL_HEADS = DIL_HPG * len(DIL_GROUPS)
DIL_WIDTH = DIL_HEADS * DIL_HEAD_DIM
DIL_OUT = DIL_HPG * DIL_HEAD_DIM
ROT_DIM = DIL_HEAD_DIM // 4
ROPE_THETA = 500000.0
SSM_HEADS = 16
SSM_P = 64
SSM_N = 128
SSM_GROUPS = 4
SSM_HPG = SSM_HEADS // SSM_GROUPS
SSM_INNER = SSM_HEADS * SSM_P
SSM_CONV_DIM = SSM_INNER + 2 * SSM_GROUPS * SSM_N
SSM_CHUNK = 64
MEM_TOKENS = 256
MEM_HEADS = 4
MEM_HEAD_DIM = 128
MEM_WIDTH = MEM_HEADS * MEM_HEAD_DIM
D_FF = 4 * D_MODEL
N_BRANCH = 3
IN_SIZES = (3 * DN_WIDTH, DN_HEADS * DN_DV, DN_HEADS, DN_HEADS, 3 * DIL_WIDTH, SSM_INNER, SSM_CONV_DIM, SSM_HEADS, N_BRANCH * D_MODEL)
N_IN = sum(IN_SIZES)
EPS = 1e-6
NEG_INF = -1e30

kernel_name = 'hybrid_deltanet_dilated_ssd_decoder_step'


def rms(x):
    xf = x.astype(jnp.float32)
    return xf * lax.rsqrt(jnp.mean(xf * xf, axis=-1, keepdims=True) + EPS)


def rmsnorm(x, g):
    return (rms(x) * g.astype(jnp.float32)).astype(x.dtype)


def l2norm(x):
    return x * lax.rsqrt(jnp.sum(x * x, axis=-1, keepdims=True) + EPS)


def split_columns(proj):
    idx = [int(i) for i in np.cumsum(IN_SIZES)[:-1]]
    return jnp.split(proj, idx, axis=-1)


def causal_conv(u, buf, w, bias=None):
    L = u.shape[1]
    full = jnp.concatenate([buf.astype(u.dtype), u], axis=1)
    y = full[:, 0:L] * w[0]
    for j in range(1, CONV_W):
        y = y + full[:, j:j + L] * w[j]
    if bias is not None:
        y = y + bias
    return jax.nn.silu(y), full[:, L:]


def to_chunks(t, c):
    b, L = t.shape[:2]
    pad = (-L) % c
    t = jnp.pad(t, [(0, 0), (0, pad)] + [(0, 0)] * (t.ndim - 2))
    t = t.reshape((b, (L + pad) // c, c) + t.shape[2:])
    return jnp.moveaxis(t, 1, 0)


def from_chunks(t, L):
    t = jnp.moveaxis(t, 0, 1)
    t = t.reshape((t.shape[0], -1) + t.shape[3:])
    return t[:, :L]


def gated_delta_rule(q, k, v, beta, g, s0):
    L = q.shape[1]
    c = min(DN_CHUNK, L)
    tri = jnp.tril(jnp.ones((c, c), bool))
    strict = jnp.tril(jnp.ones((c, c), bool), -1)
    eye = jnp.eye(c, dtype=jnp.float32)

    def step(S, inp):
        qc, kc, vc, bc, gc = inp
        gcum = jnp.cumsum(gc, axis=1)
        gh = jnp.moveaxis(gcum, 1, -1)
        diff = gh[..., :, None] - gh[..., None, :]
        decay = jnp.where(tri, jnp.exp(jnp.where(tri, diff, 0.0)), 0.0)
        kb = kc * bc[..., None]
        a_mat = jnp.einsum('bihd,bjhd->bhij', kb, kc) * jnp.where(strict, decay, 0.0)
        rhs = jnp.concatenate([vc * bc[..., None], kb * jnp.exp(gcum)[..., None]], axis=-1)
        rhs = jnp.moveaxis(rhs, 1, 2)
        sol = lax.linalg.triangular_solve(eye + a_mat, rhs, left_side=True, lower=True)
        u, w = sol[..., :DN_DV], sol[..., DN_DV:]
        v_new = u - jnp.einsum('bhcd,bhde->bhce', w, S)
        q_dec = jnp.moveaxis(qc * jnp.exp(gcum)[..., None], 1, 2)
        scores = jnp.einsum('bihd,bjhd->bhij', qc, kc) * decay
        o = jnp.einsum('bhcd,bhde->bhce', q_dec, S) + jnp.einsum('bhij,bhje->bhie', scores, v_new)
        g_last = gh[..., -1]
        k_dec = jnp.moveaxis(kc, 1, 2) * jnp.exp(g_last[..., None] - gh)[..., None]
        S = S * jnp.exp(g_last)[..., None, None] + jnp.einsum('bhcd,bhce->bhde', k_dec, v_new)
        return S, jnp.moveaxis(o, 1, 2)

    xs = tuple(to_chunks(t, c) for t in (q, k, v, beta, g))
    s_fin, o = lax.scan(step, s0, xs)
    return from_chunks(o, L), s_fin


def ssd_chunked(x, bm, cm, dt, a_neg, h0):
    L = x.shape[1]
    c = min(SSM_CHUNK, L)
    tri = jnp.tril(jnp.ones((c, c), bool))

    def step(h, inp):
        xc, bc, cc, dc = inp
        acum = jnp.cumsum(dc * a_neg, axis=1)
        ah = jnp.moveaxis(acum, 1, -1)
        diff = ah[..., :, None] - ah[..., None, :]
        lmat = jnp.where(tri, jnp.exp(jnp.where(tri, diff, 0.0)), 0.0)
        cb = jnp.einsum('bign,bjgn->bgij', cc, bc)
        wmat = cb[:, :, None] * lmat * jnp.moveaxis(dc, 1, -1)[..., None, :]
        y = jnp.einsum('bghij,bjghp->bighp', wmat, xc)
        y = y + jnp.einsum('bign,bghpn->bighp', cc, h) * jnp.exp(acum)[..., None]
        a_last = ah[..., -1]
        wdec = dc * jnp.exp(jnp.moveaxis(a_last[..., None] - ah, -1, 1))
        h = h * jnp.exp(a_last)[..., None, None] + jnp.einsum('bjgn,bjgh,bjghp->bghpn', bc, wdec, xc)
        return h, y

    xs = tuple(to_chunks(t, c) for t in (x, bm, cm, dt))
    h_fin, y = lax.scan(step, h0, xs)
    return from_chunks(y, L), h_fin


def partial_rope(t, pos):
    half = ROT_DIM // 2
    inv = jnp.power(ROPE_THETA, -jnp.arange(half, dtype=jnp.float32) * 2.0 / ROT_DIM)
    ang = pos[:, None] * inv[None, :]
    cos = jnp.cos(ang)[None, :, None, :]
    sin = jnp.sin(ang)[None, :, None, :]
    t1 = t[..., :half]
    t2 = t[..., half:ROT_DIM]
    return jnp.concatenate([t1 * cos - t2 * sin, t2 * cos + t1 * sin, t[..., ROT_DIM:]], axis=-1)


def dilated_attn_prompt(q, k, v, window, dil):
    b, S, h, e = q.shape
    span = window // dil
    M = S // dil
    nb = -(-M // span)
    mp = nb * span

    def by_residue(t):
        t = t.reshape(b, M, dil, h, e).transpose(0, 2, 1, 3, 4)
        return jnp.pad(t, ((0, 0), (0, 0), (0, mp - M), (0, 0), (0, 0)))

    def band(t):
        tp = jnp.pad(t, ((0, 0), (0, 0), (span, 0), (0, 0), (0, 0)))
        prev = tp[:, :, :mp].reshape(b, dil, nb, span, h, e)
        cur = tp[:, :, span:].reshape(b, dil, nb, span, h, e)
        return jnp.concatenate([prev, cur], axis=3)

    qb = by_residue(q).reshape(b, dil, nb, span, h, e)
    kw = band(by_residue(k))
    vw = band(by_residue(v))
    s = jnp.einsum('brnqhe,brnkhe->brnhqk', qb, kw) * (e ** -0.5)
    qi = jnp.arange(span)[:, None]
    kl = jnp.arange(2 * span)[None, :]
    dist = qi + span - kl
    blk = jnp.arange(nb)[:, None, None]
    valid = (dist >= 0) & (dist <= span) & (blk * span + kl - span >= 0)
    s = jnp.where(valid[:, None], s, NEG_INF)
    lse = jax.nn.logsumexp(s, axis=-1)
    p = jnp.exp(s - lse[..., None])
    o = jnp.einsum('brnhqk,brnkhe->brnqhe', p, vw)

    def back(t):
        t = t.reshape((b, dil, mp) + t.shape[4:])[:, :, :M]
        t = jnp.moveaxis(t, 1, 2)
        return t.reshape((b, S) + t.shape[3:])

    return back(o), back(jnp.moveaxis(lse, -1, 3))


def dilated_attn_sample(q, k, v, kbuf, vbuf, window, dil):
    T = q.shape[1]
    e = q.shape[-1]
    Lb = kbuf.shape[1]
    kall = jnp.concatenate([kbuf.astype(jnp.float32), k], axis=1)
    vall = jnp.concatenate([vbuf.astype(jnp.float32), v], axis=1)
    idx = Lb + jnp.arange(T)[:, None] - dil * jnp.arange(window // dil + 1)[None, :]
    valid = idx >= 0
    idx = jnp.maximum(idx, 0)
    kg = kall[:, idx]
    vg = vall[:, idx]
    s = jnp.einsum('bthe,btjhe->bthj', q, kg) * (e ** -0.5)
    s = jnp.where(valid[:, None, :], s, NEG_INF)
    lse = jax.nn.logsumexp(s, axis=-1)
    p = jnp.exp(s - lse[..., None])
    return jnp.einsum('bthj,btjhe->bthe', p, vg), lse


def trunk_layer(x, pos, mem_k, mem_v, dn_conv, dn_state, ssm_conv, ssm_state, win, p):
    f32 = jnp.float32
    b, L, _ = x.shape
    dt_ = x.dtype
    h = rmsnorm(x, p['norm_mix_pre'])
    (dn_qkv, dn_z, dn_b, dn_a, dil_qkv, ssm_z, ssm_xbc, ssm_dt, gates) = split_columns(h @ p['w_in'])

    qkv, dn_conv_new = causal_conv(dn_qkv, dn_conv, p['dn_conv_w'])
    qkv = qkv.astype(f32).reshape(b, L, 3, DN_HEADS, DN_DK)
    q = l2norm(qkv[:, :, 0]) * (DN_DK ** -0.5)
    k = l2norm(qkv[:, :, 1])
    v = qkv[:, :, 2]
    beta = jax.nn.sigmoid(dn_b.astype(f32))
    g = -jnp.exp(p['dn_a_log'].astype(f32)) * jax.nn.softplus(dn_a.astype(f32) + p['dn_dt_bias'].astype(f32))
    o, dn_state_new = gated_delta_rule(q, k, v, beta, g, dn_state.astype(f32))
    z = dn_z.astype(f32).reshape(b, L, DN_HEADS, DN_DV)
    o_dn = (rmsnorm(o, p['dn_norm']) * jax.nn.silu(z)).reshape(b, L, DN_WIDTH).astype(dt_)

    qkv = dil_qkv.astype(f32).reshape(b, L, 3, DIL_HEADS, DIL_HEAD_DIM)
    q = partial_rope(qkv[:, :, 0], pos)
    k = partial_rope(qkv[:, :, 1], pos)
    v = qkv[:, :, 2]
    outs, lses, win_new = [], [], []
    for gi, (window, dil) in enumerate(DIL_GROUPS):
        hs = slice(gi * DIL_HPG, (gi + 1) * DIL_HPG)
        qg, kg, vg = q[:, :, hs], k[:, :, hs], v[:, :, hs]
        if win is None:
            o, lse = dilated_attn_prompt(qg, kg, vg, window, dil)
            keep = min(window, L)
            win_new += [kg[:, L - keep:], vg[:, L - keep:]]
        else:
            o, lse = dilated_attn_sample(qg, kg, vg, win[2 * gi], win[2 * gi + 1], window, dil)
            win_new += [kg, vg]
        outs.append(o)
        lses.append(lse)
    wts = jax.nn.softmax(jnp.stack(lses), axis=0)
    o_dil = jnp.einsum('gblh,gblhe->blhe', wts, jnp.stack(outs)).reshape(b, L, DIL_OUT).astype(dt_)

    xbc, ssm_conv_new = causal_conv(ssm_xbc, ssm_conv, p['ssm_conv_w'], p['ssm_conv_b'])
    xbc = xbc.astype(f32)
    gn = SSM_GROUPS * SSM_N
    xs = xbc[..., :SSM_INNER].reshape(b, L, SSM_GROUPS, SSM_HPG, SSM_P)
    bm = xbc[..., SSM_INNER:SSM_INNER + gn].reshape(b, L, SSM_GROUPS, SSM_N)
    cm = xbc[..., SSM_INNER + gn:].reshape(b, L, SSM_GROUPS, SSM_N)
    dt = jax.nn.softplus(ssm_dt.astype(f32) + p['ssm_dt_bias'].astype(f32)).reshape(b, L, SSM_GROUPS, SSM_HPG)
    a_neg = -jnp.exp(p['ssm_a_log'].astype(f32)).reshape(SSM_GROUPS, SSM_HPG)
    h0 = ssm_state.astype(f32).reshape(b, SSM_GROUPS, SSM_HPG, SSM_P, SSM_N)
    y, ssm_state_new = ssd_chunked(xs, bm, cm, dt, a_neg, h0)
    y = y + p['ssm_d'].astype(f32).reshape(SSM_GROUPS, SSM_HPG, 1) * xs
    y = y.reshape(b, L, SSM_INNER) * jax.nn.silu(ssm_z.astype(f32))
    y = rms(y.reshape(b, L, SSM_GROUPS, SSM_INNER // SSM_GROUPS)).reshape(b, L, SSM_INNER)
    o_ssm = (y * p['ssm_norm'].astype(f32)).astype(dt_)
    ssm_state_new = ssm_state_new.reshape(b, SSM_HEADS, SSM_P, SSM_N)

    g_dn, g_dil, g_ssm = jnp.split(jax.nn.sigmoid(gates), N_BRANCH, axis=-1)
    merged = g_dn * (o_dn @ p['w_br_dn']) + g_dil * (o_dil @ p['w_br_dil']) + g_ssm * (o_ssm @ p['w_br_ssm'])
    x = x + rmsnorm(merged @ p['w_out'], p['norm_mix_post'])

    h = rmsnorm(x, p['norm_mem_pre'])
    qm = (h @ p['w_mq']).astype(f32).reshape(b, L, MEM_HEADS, MEM_HEAD_DIM)
    s = jnp.einsum('blhe,bmhe->bhlm', qm, mem_k.astype(f32)) * (MEM_HEAD_DIM ** -0.5)
    pm = jax.nn.softmax(s, axis=-1)
    om = jnp.einsum('bhlm,bmhe->blhe', pm, mem_v.astype(f32)).reshape(b, L, MEM_WIDTH).astype(dt_)
    x = x + rmsnorm(om @ p['w_mo'], p['norm_mem_post'])

    h = rmsnorm(x, p['norm_ffn_pre'])
    f = jnp.square(jax.nn.relu(h @ p['w_ff1'])) @ p['w_ff2']
    x = x + rmsnorm(f, p['norm_ffn_post'])
    states = (dn_conv_new, dn_state_new, ssm_conv_new, ssm_state_new) + tuple(win_new)
    return x, states


def setup_inputs(seed: int = 0) -> dict:
    key = jax.random.key(seed)
    keys = iter(jax.random.split(key, 64))

    def nrm(shape, scale=1.0):
        return jax.random.normal(next(keys), shape, jnp.float32) * scale

    def gain(n):
        return 1.0 + nrm((DEPTH, n), 0.05)

    def dt_bias(n):
        dt = jnp.exp(jax.random.uniform(next(keys), (DEPTH, n), jnp.float32, math.log(1e-3), math.log(1e-1)))
        return dt + jnp.log(-jnp.expm1(-dt))

    def a_log(n):
        return jnp.log(jax.random.uniform(next(keys), (DEPTH, n), jnp.float32, 1.0, 16.0))

    len1, len2, len3 = [min(w, PAST_LEN) for w, _ in DIL_GROUPS]
    kvh = (DIL_HPG, DIL_HEAD_DIM)
    memh = (MEM_TOKENS, MEM_HEADS, MEM_HEAD_DIM)
    return {
        'x_prompt': nrm((BATCH, SEQ, D_MODEL)),
        'x_sample': nrm((DEC_BATCH, DEC_SEQ, D_MODEL)),
        'state_dn_conv': nrm((DEPTH, DEC_BATCH, CONV_W - 1, 3 * DN_WIDTH)),
        'state_dn': nrm((DEPTH, DEC_BATCH, DN_HEADS, DN_DK, DN_DV), 0.1),
        'state_ssm_conv': nrm((DEPTH, DEC_BATCH, CONV_W - 1, SSM_CONV_DIM)),
        'state_ssm': nrm((DEPTH, DEC_BATCH, SSM_HEADS, SSM_P, SSM_N), 0.1),
        'cache_win1_k': nrm((DEPTH, DEC_BATCH, len1) + kvh),
        'cache_win1_v': nrm((DEPTH, DEC_BATCH, len1) + kvh),
        'cache_win2_k': nrm((DEPTH, DEC_BATCH, len2) + kvh),
        'cache_win2_v': nrm((DEPTH, DEC_BATCH, len2) + kvh),
        'cache_win3_k': nrm((DEPTH, DEC_BATCH, len3) + kvh),
        'cache_win3_v': nrm((DEPTH, DEC_BATCH, len3) + kvh),
        'cache_mem_k': nrm((DEPTH, DEC_BATCH) + memh),
        'cache_mem_v': nrm((DEPTH, DEC_BATCH) + memh),
        'mem_prompt': nrm((BATCH, MEM_TOKENS, D_MODEL)),
        'norm_mix_pre': gain(D_MODEL),
        'w_in': nrm((DEPTH, D_MODEL, N_IN), D_MODEL ** -0.5),
        'dn_conv_w': nrm((DEPTH, CONV_W, 3 * DN_WIDTH), 0.5),
        'dn_a_log': a_log(DN_HEADS),
        'dn_dt_bias': dt_bias(DN_HEADS),
        'dn_norm': gain(DN_DV),
        'ssm_conv_w': nrm((DEPTH, CONV_W, SSM_CONV_DIM), 0.5),
        'ssm_conv_b': nrm((DEPTH, SSM_CONV_DIM), 0.02),
        'ssm_a_log': a_log(SSM_HEADS),
        'ssm_dt_bias': dt_bias(SSM_HEADS),
        'ssm_d': 1.0 + nrm((DEPTH, SSM_HEADS), 0.1),
        'ssm_norm': gain(SSM_INNER),
        'w_br_dn': nrm((DEPTH, DN_WIDTH, D_MODEL), DN_WIDTH ** -0.5),
        'w_br_dil': nrm((DEPTH, DIL_OUT, D_MODEL), DIL_OUT ** -0.5),
        'w_br_ssm': nrm((DEPTH, SSM_INNER, D_MODEL), SSM_INNER ** -0.5),
        'w_out': nrm((DEPTH, D_MODEL, D_MODEL), D_MODEL ** -0.5),
        'norm_mix_post': gain(D_MODEL),
        'norm_mem_pre': gain(D_MODEL),
        'norm_mem_kv': gain(D_MODEL),
        'w_mq': nrm((DEPTH, D_MODEL, MEM_WIDTH), D_MODEL ** -0.5),
        'w_mkv': nrm((DEPTH, D_MODEL, 2 * MEM_WIDTH), D_MODEL ** -0.5),
        'w_mo': nrm((DEPTH, MEM_WIDTH, D_MODEL), MEM_WIDTH ** -0.5),
        'norm_mem_post': gain(D_MODEL),
        'norm_ffn_pre': gain(D_MODEL),
        'w_ff1': nrm((DEPTH, D_MODEL, D_FF), D_MODEL ** -0.5),
        'w_ff2': nrm((DEPTH, D_FF, D_MODEL), D_FF ** -0.5),
        'norm_ffn_post': gain(D_MODEL),
    }


def reference(x_prompt, x_sample, state_dn_conv, state_dn, state_ssm_conv, state_ssm,
              cache_win1_k, cache_win1_v, cache_win2_k, cache_win2_v, cache_win3_k, cache_win3_v,
              cache_mem_k, cache_mem_v, mem_prompt,
              norm_mix_pre, w_in, dn_conv_w, dn_a_log, dn_dt_bias, dn_norm,
              ssm_conv_w, ssm_conv_b, ssm_a_log, ssm_dt_bias, ssm_d, ssm_norm,
              w_br_dn, w_br_dil, w_br_ssm, w_out, norm_mix_post,
              norm_mem_pre, norm_mem_kv, w_mq, w_mkv, w_mo, norm_mem_post,
              norm_ffn_pre, w_ff1, w_ff2, norm_ffn_post):
    n_p, S = x_prompt.shape[:2]
    T = x_sample.shape[1]
    n_mem = mem_prompt.shape[1]
    pos_p = jnp.arange(S, dtype=jnp.float32)
    pos_s = PAST_LEN + jnp.arange(T, dtype=jnp.float32)
    zero_dn_conv = jnp.zeros((n_p, CONV_W - 1, 3 * DN_WIDTH), x_prompt.dtype)
    zero_dn = jnp.zeros((n_p, DN_HEADS, DN_DK, DN_DV), jnp.float32)
    zero_ssm_conv = jnp.zeros((n_p, CONV_W - 1, SSM_CONV_DIM), x_prompt.dtype)
    zero_ssm = jnp.zeros((n_p, SSM_HEADS, SSM_P, SSM_N), jnp.float32)
    xp, xs = x_prompt, x_sample
    new_p = [[] for _ in range(12)]
    new_s = [[] for _ in range(10)]
    for l in range(DEPTH):
        prm = dict(norm_mix_pre=norm_mix_pre[l], w_in=w_in[l], dn_conv_w=dn_conv_w[l], dn_a_log=dn_a_log[l],
                   dn_dt_bias=dn_dt_bias[l], dn_norm=dn_norm[l], ssm_conv_w=ssm_conv_w[l], ssm_conv_b=ssm_conv_b[l],
                   ssm_a_log=ssm_a_log[l], ssm_dt_bias=ssm_dt_bias[l], ssm_d=ssm_d[l], ssm_norm=ssm_norm[l],
                   w_br_dn=w_br_dn[l], w_br_dil=w_br_dil[l], w_br_ssm=w_br_ssm[l], w_out=w_out[l],
                   norm_mix_post=norm_mix_post[l], norm_mem_pre=norm_mem_pre[l], w_mq=w_mq[l], w_mo=w_mo[l],
                   norm_mem_post=norm_mem_post[l], norm_ffn_pre=norm_ffn_pre[l], w_ff1=w_ff1[l], w_ff2=w_ff2[l],
                   norm_ffn_post=norm_ffn_post[l])
        mkv = (rmsnorm(mem_prompt, norm_mem_kv[l]) @ w_mkv[l]).reshape(n_p, n_mem, 2, MEM_HEADS, MEM_HEAD_DIM)
        mk, mv = mkv[:, :, 0], mkv[:, :, 1]
        xp, st_p = trunk_layer(xp, pos_p, mk, mv, zero_dn_conv, zero_dn, zero_ssm_conv, zero_ssm, None, prm)
        for i, a in enumerate(st_p + (mk, mv)):
            new_p[i].append(a)
        win_l = (cache_win1_k[l], cache_win1_v[l], cache_win2_k[l], cache_win2_v[l], cache_win3_k[l], cache_win3_v[l])
        xs, st_s = trunk_layer(xs, pos_s, cache_mem_k[l], cache_mem_v[l], state_dn_conv[l], state_dn[l],
                               state_ssm_conv[l], state_ssm[l], win_l, prm)
        for i, a in enumerate(st_s):
            new_s[i].append(a)
    (p_dn_conv, p_dn, p_ssm_conv, p_ssm, p_win1_k, p_win1_v, p_win2_k, p_win2_v,
     p_win3_k, p_win3_v, p_mem_k, p_mem_v) = [jnp.stack(a) for a in new_p]
    (s_dn_conv, s_dn, s_ssm_conv, s_ssm, s_win1_k, s_win1_v, s_win2_k, s_win2_v,
     s_win3_k, s_win3_v) = [jnp.stack(a) for a in new_s]
    return (xp, xs, p_dn_conv, p_dn, p_ssm_conv, p_ssm, p_win1_k, p_win1_v, p_win2_k, p_win2_v,
            p_win3_k, p_win3_v, p_mem_k, p_mem_v, s_dn_conv, s_dn, s_ssm_conv, s_ssm,
            s_win1_k, s_win1_v, s_win2_k, s_win2_v, s_win3_k, s_win3_v)
```

```python
import functools
import math

import jax
import jax.numpy as jnp
from jax import lax
from jax.experimental import pallas as pl
from jax.experimental.pallas import tpu as pltpu

F32 = jnp.float32
BF16 = jnp.bfloat16

D_MODEL = 1024
CONV_W = 4
DN_HEADS = 8
DN_DK = 128
DN_WIDTH = DN_HEADS * DN_DK
DIL_GROUPS = ((128, 1), (512, 4), (2048, 16))
DIL_HPG = 4
HEAD = 128
DIL_HEADS = DIL_HPG * len(DIL_GROUPS)
DIL_WIDTH = DIL_HEADS * HEAD
DIL_OUT = DIL_HPG * HEAD
ROT_DIM = HEAD // 4
ROPE_THETA = 500000.0
PAST_LEN = 2048
SSM_HEADS = 16
SSM_P = 64
SSM_N = 128
SSM_GROUPS = 4
SSM_HPG = SSM_HEADS // SSM_GROUPS
SSM_INNER = SSM_HEADS * SSM_P
SSM_CONV_DIM = SSM_INNER + 2 * SSM_GROUPS * SSM_N
MEM_HEADS = 4
MEM_WIDTH = MEM_HEADS * HEAD
D_FF = 4 * D_MODEL
EPS = 1e-6
NEG_INF = -1e30
IN_SIZES = (3 * DN_WIDTH, DN_WIDTH, DN_HEADS, DN_HEADS, 3 * DIL_WIDTH, SSM_INNER, SSM_CONV_DIM, SSM_HEADS,
            3 * D_MODEL)

OFF_DNQKV = 0
OFF_DNZ = OFF_DNQKV + 3 * DN_WIDTH
OFF_XBC = OFF_DNZ + DN_WIDTH
OFF_SSMZ = OFF_XBC + SSM_CONV_DIM
OFF_GATES = OFF_SSMZ + SSM_INNER
OFF_DILQ = OFF_GATES + 3 * D_MODEL
OFF_DILK = OFF_DILQ + DIL_WIDTH
OFF_DILV = OFF_DILK + DIL_WIDTH
OFF_SMALL = OFF_DILV + DIL_WIDTH
SMALL_W = 128
NP = 15 * 1024
LANE_DN_B = 0
LANE_DN_A = DN_HEADS
LANE_SSM_DT = 2 * DN_HEADS

SOLVE_BASE = 16

VMEM_LIMIT = 56 * 1024 * 1024


def _cparams(sem):
    return pltpu.CompilerParams(dimension_semantics=sem, vmem_limit_bytes=VMEM_LIMIT)


def _dot(a, b):
    return jnp.dot(a.astype(BF16), b.astype(BF16), preferred_element_type=F32)


def _dot_nt(a, b):
    return lax.dot_general(a.astype(BF16), b.astype(BF16), (((1,), (1,)), ((), ())), preferred_element_type=F32)


def _dot_tn(a, b):
    return lax.dot_general(a.astype(BF16), b.astype(BF16), (((0,), (0,)), ((), ())), preferred_element_type=F32)


def _split3(x):
    hi = x.astype(BF16)
    r = x - hi.astype(F32)
    mid = r.astype(BF16)
    lo = (r - mid.astype(F32)).astype(BF16)
    return hi, mid, lo


def _sel_dot(sel, x):
    return sum(jnp.dot(sel, p, preferred_element_type=F32) for p in _split3(x))


def _sel_dot_nt(sel, x):
    return sum(lax.dot_general(sel, p, (((1,), (1,)), ((), ())), preferred_element_type=F32) for p in _split3(x))


def _rms_scale(x):
    return x * lax.rsqrt(jnp.mean(x * x, axis=-1, keepdims=True) + EPS)


def _softplus(x):
    return jnp.maximum(x, 0.0) + jnp.log1p(jnp.exp(-jnp.abs(x)))


def _sigmoid(x):
    return 1.0 / (1.0 + jnp.exp(-x))


def _silu(x):
    return x * _sigmoid(x)


def _iota2(shape, axis):
    return lax.broadcasted_iota(jnp.int32, shape, axis)


def _norm_matmul_kernel(*refs, rope_tiles, tn):
    if rope_tiles is None:
        x_ref, g_ref, w_ref, o_ref, h_ref = refs
    else:
        x_ref, g_ref, w_ref, cos_ref, sa_ref, sb_ref, o_ref, h_ref = refs
    j = pl.program_id(1)

    @pl.when(j == 0)
    def _():
        h_ref[...] = (_rms_scale(x_ref[...]) * g_ref[...]).astype(BF16)

    y = jnp.dot(h_ref[...], w_ref[...], preferred_element_type=F32)
    if rope_tiles is None:
        o_ref[...] = y
        return
    lo, hi = rope_tiles
    is_rope = jnp.logical_and(j >= lo, j < hi)

    @pl.when(is_rope)
    def _():
        cos, sa, sb = cos_ref[...], sa_ref[...], sb_ref[...]
        for c in range(tn // HEAD):
            t = y[:, c * HEAD:(c + 1) * HEAD]
            o_ref[:, c * HEAD:(c + 1) * HEAD] = (
                t * cos + pltpu.roll(t, HEAD - ROT_DIM // 2, 1) * sa + pltpu.roll(t, ROT_DIM // 2, 1) * sb)

    @pl.when(jnp.logical_not(is_rope))
    def _():
        o_ref[...] = y


def norm_matmul(x, g, w, *, tm, tn, rope=None):
    n, k = x.shape
    m = w.shape[1]
    assert n % tm == 0 and m % tn == 0
    in_specs = [pl.BlockSpec((tm, k), lambda i, j: (i, 0)),
                pl.BlockSpec((1, k), lambda i, j: (0, 0)),
                pl.BlockSpec((k, tn), lambda i, j: (0, j))]
    args = [x, g.reshape(1, k), w]
    rope_tiles = None
    if rope is not None:
        cos, sa, sb, lo, hi = rope
        period = cos.shape[0] // tm
        assert cos.shape[0] % tm == 0
        tab = pl.BlockSpec((tm, HEAD), lambda i, j: (i % period, 0))
        in_specs += [tab, tab, tab]
        args += [cos, sa, sb]
        rope_tiles = (lo, hi)
    return pl.pallas_call(
        functools.partial(_norm_matmul_kernel, rope_tiles=rope_tiles, tn=tn),
        grid=(n // tm, m // tn),
        in_specs=in_specs,
        out_specs=pl.BlockSpec((tm, tn), lambda i, j: (i, j)),
        out_shape=jax.ShapeDtypeStruct((n, m), F32),
        scratch_shapes=[pltpu.VMEM((tm, k), BF16)],
        compiler_params=_cparams(("parallel", "arbitrary")),
        name="norm_matmul",
    )(*args)


def rope_tables(pos):
    half = ROT_DIM // 2
    inv = jnp.power(ROPE_THETA, -jnp.arange(half, dtype=F32) * 2.0 / ROT_DIM)
    ang = pos[:, None] * inv[None, :]
    cos, sin = jnp.cos(ang), jnp.sin(ang)
    n = pos.shape[0]
    ones = jnp.ones((n, HEAD - ROT_DIM), F32)
    zeros = jnp.zeros((n, HEAD - ROT_DIM), F32)
    z16 = jnp.zeros((n, half), F32)
    return (jnp.concatenate([cos, cos, ones], axis=1),
            jnp.concatenate([-sin, z16, zeros], axis=1),
            jnp.concatenate([z16, sin, zeros], axis=1))


def _conv_cols(xbuf, cw_ref, c0, width, rows, bias_ref=None):
    acc = xbuf[5:5 + rows, c0:c0 + width] * cw_ref[0:1, c0:c0 + width]
    for j in range(1, CONV_W):
        acc = acc + xbuf[5 + j:5 + j + rows, c0:c0 + width] * cw_ref[j:j + 1, c0:c0 + width]
    if bias_ref is not None:
        acc = acc + bias_ref[0:1, c0:c0 + width]
    return _silu(acc)


def _stage_rows(xbuf, smbuf, parts, sm_ref, cs_ref, rows_in, rows):
    @pl.when(pl.program_id(1) == 0)
    def _():
        xbuf[5:8, :] = cs_ref[...]

    @pl.when(pl.program_id(1) > 0)
    def _():
        xbuf[5:8, :] = xbuf[5 + rows_in:8 + rows_in, :]

    for ref, c0 in parts:
        xbuf[8:8 + rows_in, c0:c0 + ref.shape[1]] = ref[...]
    if rows_in < rows:
        xbuf[8 + rows_in:8 + rows, :] = jnp.zeros((rows - rows_in, xbuf.shape[1]), F32)
        smbuf[...] = jnp.zeros(smbuf.shape, F32)
        smbuf[0:rows_in, :] = sm_ref[...]
        return smbuf[...]
    return sm_ref[...]


def _dn_kernel(q_ref, k_ref, v_ref, z_ref, sm_ref, cs_ref, cw_ref, alog_ref, dtb_ref, gam_ref, s0_ref,
               o_ref, st_ref, xbuf, smbuf, *, rows_in, rows):
    c = rows

    @pl.when(pl.program_id(1) == 0)
    def _():
        st_ref[...] = s0_ref[...]

    sm = _stage_rows(xbuf, smbuf, ((q_ref, 0), (k_ref, DN_WIDTH), (v_ref, 2 * DN_WIDTH)), sm_ref, cs_ref,
                     rows_in, rows)
    beta_all = _sigmoid(sm)
    g_all = -jnp.exp(alog_ref[...]) * _softplus(sm + dtb_ref[...])
    if rows_in < rows:
        live = (_iota2((c, 1), 0) < rows_in).astype(F32)
        beta_all = beta_all * live
        g_all = g_all * live
    ii = _iota2((c, c), 0)
    jj = _iota2((c, c), 1)
    tri = ii >= jj
    strict = ii > jj
    eye = (ii == jj).astype(F32)
    gcum_all = _sel_dot(tri.astype(BF16), g_all)
    sel = (_iota2((DN_HEADS, SMALL_W), 1) == _iota2((DN_HEADS, SMALL_W), 0) + LANE_DN_A).astype(BF16)
    grow_all = _sel_dot_nt(sel, gcum_all)
    base = min(c, SOLVE_BASE)
    n_sq = int(math.log2(base)) - 1
    n_lvl = int(math.log2(c // base))
    assert 2 ** (n_sq + 1) == base and base * 2 ** n_lvl == c
    same_blk = [(ii >> (n_sq + 1 + lvl)) == (jj >> (n_sq + 1 + lvl)) for lvl in range(n_lvl + 1)]

    for h in range(DN_HEADS):
        q = _conv_cols(xbuf, cw_ref, h * DN_DK, DN_DK, c)
        k = _conv_cols(xbuf, cw_ref, DN_WIDTH + h * DN_DK, DN_DK, c)
        v = _conv_cols(xbuf, cw_ref, 2 * DN_WIDTH + h * DN_DK, DN_DK, c)
        q = q * lax.rsqrt(jnp.sum(q * q, axis=-1, keepdims=True) + EPS) * (DN_DK ** -0.5)
        k = k * lax.rsqrt(jnp.sum(k * k, axis=-1, keepdims=True) + EPS)
        gc = gcum_all[:, LANE_DN_A + h:LANE_DN_A + h + 1]
        gr = grow_all[h:h + 1, :]
        beta = beta_all[:, LANE_DN_B + h:LANE_DN_B + h + 1]
        decay = jnp.where(tri, jnp.exp(jnp.where(tri, gc - gr, 0.0)), 0.0)
        eg = jnp.exp(gc)
        kb = k * beta
        a_mat = _dot_nt(kb, k) * jnp.where(strict, decay, 0.0)
        x = -jnp.where(same_blk[0], a_mat, 0.0)
        t_inv = eye + x
        for _ in range(n_sq):
            x = _dot(x, x)
            t_inv = t_inv + _dot(t_inv, x)
        for lvl in range(1, len(same_blk)):
            below = jnp.where(jnp.logical_and(same_blk[lvl], jnp.logical_not(same_blk[lvl - 1])), a_mat, 0.0)
            t_inv = t_inv - _dot(_dot(t_inv, below), t_inv)
        sol = _dot(t_inv, jnp.concatenate([v * beta, kb * eg], axis=1))
        u, w = sol[:, :DN_DK], sol[:, DN_DK:]
        s_prev = st_ref[h]
        v_new = u - _dot(w, s_prev)
        o = _dot(q * eg, s_prev) + _dot(_dot_nt(q, k) * decay, v_new)
        g_last = gc[c - 1:c, :]
        k_dec = k * jnp.exp(g_last - gc)
        st_ref[h] = s_prev * jnp.exp(g_last) + _dot_tn(k_dec, v_new)
        z = z_ref[:, h * DN_DK:(h + 1) * DN_DK]
        o_ref[:, h * DN_DK:(h + 1) * DN_DK] = _rms_scale(o[0:rows_in, :]) * gam_ref[...] * _silu(z)


def deltanet(proj, conv_state, state, conv_w, a_log, dt_bias, norm_g, *, rows_in, rows):
    b, l, _ = proj.shape
    assert l % rows_in == 0 and rows_in <= rows
    nc = l // rows_in
    lane_vec = lambda v, lane0: jnp.zeros((1, SMALL_W), F32).at[0, lane0:lane0 + v.shape[0]].set(v)
    blk = lambda w, off: pl.BlockSpec((None, rows_in, w), lambda i, j: (i, j, off // w))
    const = lambda shape: pl.BlockSpec(shape, lambda i, j: (0,) * len(shape))
    o, st = pl.pallas_call(
        functools.partial(_dn_kernel, rows_in=rows_in, rows=rows),
        grid=(b, nc),
        in_specs=[blk(DN_WIDTH, OFF_DNQKV), blk(DN_WIDTH, OFF_DNQKV + DN_WIDTH), blk(DN_WIDTH, OFF_DNQKV + 2 * DN_WIDTH),
                  blk(DN_WIDTH, OFF_DNZ), blk(SMALL_W, OFF_SMALL),
                  pl.BlockSpec((None, CONV_W - 1, 3 * DN_WIDTH), lambda i, j: (i, 0, 0)),
                  const((CONV_W, 3 * DN_WIDTH)), const((1, SMALL_W)), const((1, SMALL_W)), const((1, DN_DK)),
                  pl.BlockSpec((None, DN_HEADS, DN_DK, DN_DK), lambda i, j: (i, 0, 0, 0))],
        out_specs=[pl.BlockSpec((None, rows_in, DN_WIDTH), lambda i, j: (i, j, 0)),
                   pl.BlockSpec((None, DN_HEADS, DN_DK, DN_DK), lambda i, j: (i, 0, 0, 0))],
        out_shape=[jax.ShapeDtypeStruct((b, l, DN_WIDTH), F32),
                   jax.ShapeDtypeStruct((b, DN_HEADS, DN_DK, DN_DK), F32)],
        scratch_shapes=[pltpu.VMEM((8 + rows, 3 * DN_WIDTH), F32), pltpu.VMEM((rows, SMALL_W), F32)],
        compiler_params=_cparams(("parallel", "arbitrary")),
        name="deltanet",
    )(proj, proj, proj, proj, proj, conv_state, conv_w, lane_vec(a_log, LANE_DN_A), lane_vec(dt_bias, LANE_DN_A),
      norm_g.reshape(1, DN_DK), state)
    return o, st


def _ssd_kernel(x_ref, z_ref, sm_ref, cs_ref, cw_ref, cb_ref, alog_ref, dtb_ref, d_ref, gam_ref, h0_ref,
                o_ref, st_ref, xbuf, smbuf, *, rows_in, rows):
    c = rows

    @pl.when(pl.program_id(1) == 0)
    def _():
        st_ref[...] = h0_ref[...]

    sm = _stage_rows(xbuf, smbuf, ((x_ref, 0),), sm_ref, cs_ref, rows_in, rows)
    dt_all = _softplus(sm + dtb_ref[...])
    if rows_in < rows:
        dt_all = dt_all * (_iota2((c, 1), 0) < rows_in).astype(F32)
    da_all = dt_all * (-jnp.exp(alog_ref[...]))
    ii = _iota2((c, c), 0)
    jj = _iota2((c, c), 1)
    tri = ii >= jj
    acum_all = _sel_dot(tri.astype(BF16), da_all)
    sel = (_iota2((SSM_HEADS, SMALL_W), 1) == _iota2((SSM_HEADS, SMALL_W), 0) + LANE_SSM_DT).astype(BF16)
    arow_all = _sel_dot_nt(sel, acum_all)
    dtrow_all = _sel_dot_nt(sel, dt_all)
    gn = SSM_GROUPS * SSM_N
    gw = SSM_HPG * SSM_P

    for g in range(SSM_GROUPS):
        bm = _conv_cols(xbuf, cw_ref, SSM_INNER + g * SSM_N, SSM_N, c, cb_ref)
        cm = _conv_cols(xbuf, cw_ref, SSM_INNER + gn + g * SSM_N, SSM_N, c, cb_ref)
        xg = _conv_cols(xbuf, cw_ref, g * gw, gw, c, cb_ref)
        cb = _dot_nt(cm, bm)
        ys = []
        for hh in range(SSM_HPG):
            h = g * SSM_HPG + hh
            xh = xg[:, hh * SSM_P:(hh + 1) * SSM_P]
            ac = acum_all[:, LANE_SSM_DT + h:LANE_SSM_DT + h + 1]
            dc = dt_all[:, LANE_SSM_DT + h:LANE_SSM_DT + h + 1]
            ar = arow_all[h:h + 1, :]
            dr = dtrow_all[h:h + 1, :]
            lmat = jnp.where(tri, jnp.exp(jnp.where(tri, ac - ar, 0.0)), 0.0)
            h_prev = st_ref[h]
            y = _dot(cb * lmat * dr, xh) + _dot_nt(cm, h_prev) * jnp.exp(ac)
            a_last = ac[c - 1:c, :]
            wdec = dc * jnp.exp(a_last - ac)
            st_ref[h] = h_prev * jnp.exp(a_last) + _dot_tn(xh * wdec, bm)
            ys.append(y + d_ref[0:1, h:h + 1] * xh)
        yg = jnp.concatenate(ys, axis=1)
        z = z_ref[:, g * gw:(g + 1) * gw]
        o_ref[:, g * gw:(g + 1) * gw] = _rms_scale(yg[0:rows_in, :] * _silu(z)) * gam_ref[0:1, g * gw:(g + 1) * gw]


def ssd(proj, conv_state, state, conv_w, conv_b, a_log, dt_bias, d_skip, norm_g, *, rows_in, rows):
    b, l, _ = proj.shape
    nc = l // rows_in
    lane_vec = lambda v, lane0: jnp.zeros((1, SMALL_W), F32).at[0, lane0:lane0 + v.shape[0]].set(v)
    blk = lambda w, off: pl.BlockSpec((None, rows_in, w), lambda i, j: (i, j, off // w))
    const = lambda shape: pl.BlockSpec(shape, lambda i, j: (0,) * len(shape))
    o, st = pl.pallas_call(
        functools.partial(_ssd_kernel, rows_in=rows_in, rows=rows),
        grid=(b, nc),
        in_specs=[blk(SSM_CONV_DIM, OFF_XBC), blk(SSM_INNER, OFF_SSMZ), blk(SMALL_W, OFF_SMALL),
                  pl.BlockSpec((None, CONV_W - 1, SSM_CONV_DIM), lambda i, j: (i, 0, 0)),
                  const((CONV_W, SSM_CONV_DIM)), const((1, SSM_CONV_DIM)), const((1, SMALL_W)), const((1, SMALL_W)),
                  const((1, SMALL_W)), const((1, SSM_INNER)),
                  pl.BlockSpec((None, SSM_HEADS, SSM_P, SSM_N), lambda i, j: (i, 0, 0, 0))],
        out_specs=[pl.BlockSpec((None, rows_in, SSM_INNER), lambda i, j: (i, j, 0)),
                   pl.BlockSpec((None, SSM_HEADS, SSM_P, SSM_N), lambda i, j: (i, 0, 0, 0))],
        out_shape=[jax.ShapeDtypeStruct((b, l, SSM_INNER), F32),
                   jax.ShapeDtypeStruct((b, SSM_HEADS, SSM_P, SSM_N), F32)],
        scratch_shapes=[pltpu.VMEM((8 + rows, SSM_CONV_DIM), F32), pltpu.VMEM((rows, SMALL_W), F32)],
        compiler_params=_cparams(("parallel", "arbitrary")),
        name="ssd",
    )(proj, proj, proj, conv_state, conv_w, conv_b.reshape(1, SSM_CONV_DIM), lane_vec(a_log, LANE_SSM_DT),
      lane_vec(dt_bias, LANE_SSM_DT), lane_vec(d_skip, 0), norm_g.reshape(1, SSM_INNER), state)
    return o, st


def _dil_prompt_kernel(q_ref, k_ref, v_ref, o_ref, lse_ref, *, nb):
    span = HEAD
    scale = HEAD ** -0.5
    ii = _iota2((span, span), 0)
    jj = _iota2((span, span), 1)
    cur_ok = jj <= ii
    prev_ok = jj >= ii

    def block(n, with_prev):
        st = n * span if isinstance(n, int) else pl.multiple_of(n * span, span)
        q = q_ref[pl.ds(st, span), :]
        k = k_ref[pl.ds(st, span), :]
        v = v_ref[pl.ds(st, span), :]
        s_c = jnp.where(cur_ok, _dot_nt(q, k) * scale, NEG_INF)
        m = jnp.max(s_c, axis=-1, keepdims=True)
        if with_prev:
            sp = st - span
            kp = k_ref[pl.ds(sp, span), :]
            vp = v_ref[pl.ds(sp, span), :]
            s_p = jnp.where(prev_ok, _dot_nt(q, kp) * scale, NEG_INF)
            m = jnp.maximum(m, jnp.max(s_p, axis=-1, keepdims=True))
            p_p = jnp.exp(s_p - m)
        p_c = jnp.exp(s_c - m)
        l = jnp.sum(p_c, axis=-1, keepdims=True)
        acc = _dot(p_c, v)
        if with_prev:
            l = l + jnp.sum(p_p, axis=-1, keepdims=True)
            acc = acc + _dot(p_p, vp)
        o_ref[pl.ds(st, span), :] = acc / l
        lse_ref[pl.ds(st, span), :] = jnp.broadcast_to(m + jnp.log(l), (span, HEAD))

    block(0, False)
    if nb > 1:
        def body(n, carry):
            block(n, True)
            return carry
        lax.fori_loop(1, nb, body, 0)


def dilated_prompt_group(proj, gi, dil):
    b, s, _ = proj.shape
    m = s // dil
    assert s % dil == 0 and m % HEAD == 0
    view = proj.reshape(b, m, dil * NP)
    npb = NP // HEAD

    def src(off):
        base = off // HEAD + gi * DIL_HPG
        return pl.BlockSpec((None, m, HEAD), lambda i, h, r: (i, 0, r * npb + base + h))

    dst = pl.BlockSpec((None, m, HEAD), lambda i, h, r: (i, 0, r * DIL_HPG + h))
    o, lse = pl.pallas_call(
        functools.partial(_dil_prompt_kernel, nb=m // HEAD),
        grid=(b, DIL_HPG, dil),
        in_specs=[src(OFF_DILQ), src(OFF_DILK), src(OFF_DILV)],
        out_specs=[dst, dst],
        out_shape=[jax.ShapeDtypeStruct((b, m, dil * DIL_OUT), F32)] * 2,
        compiler_params=_cparams(("parallel", "parallel", "parallel")),
        name="dilated_prompt",
    )(view, view, view)
    return o.reshape(b, s, DIL_OUT), lse.reshape(b, s, DIL_OUT)


def _dil_sample_kernel(*refs, t_new, rows_per_group):
    qs, ks, vs = refs[0:3], refs[3:6], refs[6:9]
    caches = refs[9:15]
    o_ref = refs[15]
    scale = HEAD ** -0.5
    ng = len(DIL_GROUPS)
    for t in range(t_new):
        for h in range(DIL_HPG):
            outs, lses = [], []
            for g, (_, dil) in enumerate(DIL_GROUPS):
                kc, vc = caches[2 * g], caches[2 * g + 1]
                nrow = rows_per_group[g]
                cols = slice(h * HEAD, (h + 1) * HEAD)
                ccols = slice((t % dil) * DIL_OUT + h * HEAD, (t % dil) * DIL_OUT + (h + 1) * HEAD)
                qrow = qs[g][t:t + 1, cols]
                s_c = jnp.sum(kc[:, ccols] * qrow, axis=-1, keepdims=True) * scale
                first = -(-(t - t % dil) // dil)
                if first > 0:
                    s_c = jnp.where(_iota2((nrow, 1), 0) >= first, s_c, NEG_INF)
                s_n = jnp.sum(ks[g][:, cols] * qrow, axis=-1, keepdims=True) * scale
                tn = _iota2((t_new, 1), 0)
                new_ok = functools.reduce(jnp.logical_or, [tn == t - dil * j for j in range(t // dil + 1)])
                s_n = jnp.where(new_ok, s_n, NEG_INF)
                m = jnp.maximum(jnp.max(s_c, axis=0, keepdims=True), jnp.max(s_n, axis=0, keepdims=True))
                p_c = jnp.exp(s_c - m)
                p_n = jnp.exp(s_n - m)
                l = jnp.sum(p_c, axis=0, keepdims=True) + jnp.sum(p_n, axis=0, keepdims=True)
                acc = (jnp.sum(p_c * vc[:, ccols], axis=0, keepdims=True)
                       + jnp.sum(p_n * vs[g][:, cols], axis=0, keepdims=True))
                outs.append(acc / l)
                lses.append(m + jnp.log(l))
            top = functools.reduce(jnp.maximum, lses)
            wts = [jnp.exp(x - top) for x in lses]
            num = sum(w * o for w, o in zip(wts, outs))
            o_ref[t:t + 1, h * HEAD:(h + 1) * HEAD] = num / sum(wts)


def dilated_sample(proj, caches):
    b, t_new, _ = proj.shape
    views, specs, rows_per_group = [], [], []
    for g, (window, dil) in enumerate(DIL_GROUPS):
        for c in caches[2 * g:2 * g + 2]:
            lb = c.shape[1]
            assert lb == window and lb % dil == 0 and (dil == 1 or t_new <= dil)
            rows = lb // dil
            views.append(c.reshape(b, rows, dil * DIL_OUT))
            width = min(dil, t_new) * DIL_OUT
            specs.append(pl.BlockSpec((None, rows, width), lambda i: (i, 0, 0)))
        rows_per_group.append(lb // dil)
    blk = lambda off, g: pl.BlockSpec((None, t_new, DIL_OUT), lambda i: (i, 0, off // DIL_OUT + g))
    in_specs = ([blk(OFF_DILQ, g) for g in range(3)] + [blk(OFF_DILK, g) for g in range(3)]
                + [blk(OFF_DILV, g) for g in range(3)] + specs)
    return pl.pallas_call(
        functools.partial(_dil_sample_kernel, t_new=t_new, rows_per_group=tuple(rows_per_group)),
        grid=(b,),
        in_specs=in_specs,
        out_specs=pl.BlockSpec((None, t_new, DIL_OUT), lambda i: (i, 0, 0)),
        out_shape=jax.ShapeDtypeStruct((b, t_new, DIL_OUT), F32),
        compiler_params=_cparams(("parallel",)),
        name="dilated_sample",
    )(*([proj] * 9), *views)


def _merge_kernel(*refs, n_parts):
    (x_ref, odn_ref, ossm_ref), rest = refs[:3], refs[3:]
    dil_refs, rest = rest[:n_parts], rest[n_parts:]
    g_dn, g_dil, g_ssm, w_dn, w_dil, w_ssm, w_out, gam_ref, o_ref = rest
    if n_parts == 1:
        o_dil = dil_refs[0][...]
    else:
        outs, lses = dil_refs[:n_parts // 2], dil_refs[n_parts // 2:]
        top = functools.reduce(jnp.maximum, [r[...] for r in lses])
        wts = [jnp.exp(r[...] - top) for r in lses]
        o_dil = sum(w * r[...] for w, r in zip(wts, outs)) / sum(wts)
    merged = (_sigmoid(g_dn[...]) * _dot(odn_ref[...], w_dn[...])
              + _sigmoid(g_dil[...]) * _dot(o_dil, w_dil[...])
              + _sigmoid(g_ssm[...]) * _dot(ossm_ref[...], w_ssm[...]))
    y = _dot(merged, w_out[...])
    o_ref[...] = x_ref[...] + _rms_scale(y) * gam_ref[...]


def branch_merge(x, o_dn, o_ssm, dil_parts, proj, w_dn, w_dil, w_ssm, w_out, gam, *, tm):
    n = x.shape[0]
    row = lambda w: pl.BlockSpec((tm, w), lambda i: (i, 0))
    gate = lambda k: pl.BlockSpec((tm, D_MODEL), lambda i: (i, OFF_GATES // D_MODEL + k))
    const = lambda a: pl.BlockSpec(a.shape, lambda i: (0,) * a.ndim)
    gam = gam.reshape(1, D_MODEL)
    return pl.pallas_call(
        functools.partial(_merge_kernel, n_parts=len(dil_parts)),
        grid=(n // tm,),
        in_specs=[row(D_MODEL), row(DN_WIDTH), row(SSM_INNER)] + [row(DIL_OUT)] * len(dil_parts)
        + [gate(0), gate(1), gate(2), const(w_dn), const(w_dil), const(w_ssm), const(w_out), const(gam)],
        out_specs=row(D_MODEL),
        out_shape=jax.ShapeDtypeStruct((n, D_MODEL), F32),
        compiler_params=_cparams(("parallel",)),
        name="branch_merge",
    )(x, o_dn, o_ssm, *dil_parts, proj, proj, proj, w_dn, w_dil, w_ssm, w_out, gam)


def _mem_attn_kernel(q_ref, k_ref, v_ref, o_ref, qbuf, *, rows_in, rows):
    scale = HEAD ** -0.5
    if rows_in < rows:
        qbuf[...] = jnp.zeros(qbuf.shape, F32)
        qbuf[0:rows_in, :] = q_ref[...]
        q_all = qbuf
    else:
        q_all = q_ref
    for h in range(MEM_HEADS):
        cols = slice(h * HEAD, (h + 1) * HEAD)
        s = _dot_nt(q_all[:, cols], k_ref[:, cols]) * scale
        m = jnp.max(s, axis=-1, keepdims=True)
        p = jnp.exp(s - m)
        o = _dot(p, v_ref[:, cols]) / jnp.sum(p, axis=-1, keepdims=True)
        o_ref[:, cols] = o[0:rows_in, :]


def mem_attention(q, k, v, k_col, v_col, *, tq):
    b, l, _ = q.shape
    n_mem = k.shape[1]
    rows = max(tq, 8)
    return pl.pallas_call(
        functools.partial(_mem_attn_kernel, rows_in=tq, rows=rows),
        grid=(b, l // tq),
        in_specs=[pl.BlockSpec((None, tq, MEM_WIDTH), lambda i, j: (i, j, 0)),
                  pl.BlockSpec((None, n_mem, MEM_WIDTH), lambda i, j: (i, 0, k_col)),
                  pl.BlockSpec((None, n_mem, MEM_WIDTH), lambda i, j: (i, 0, v_col))],
        out_specs=pl.BlockSpec((None, tq, MEM_WIDTH), lambda i, j: (i, j, 0)),
        out_shape=jax.ShapeDtypeStruct((b, l, MEM_WIDTH), F32),
        scratch_shapes=[pltpu.VMEM((rows, MEM_WIDTH), F32)],
        compiler_params=_cparams(("parallel", "parallel")),
        name="mem_attention",
    )(q, k, v)


def _matmul_norm_res_kernel(x_ref, a_ref, w_ref, gam_ref, o_ref):
    y = _dot(a_ref[...], w_ref[...])
    o_ref[...] = x_ref[...] + _rms_scale(y) * gam_ref[...]


def matmul_norm_residual(x, a, w, gam, *, tm):
    n, k = a.shape
    return pl.pallas_call(
        _matmul_norm_res_kernel,
        grid=(n // tm,),
        in_specs=[pl.BlockSpec((tm, D_MODEL), lambda i: (i, 0)), pl.BlockSpec((tm, k), lambda i: (i, 0)),
                  pl.BlockSpec((k, D_MODEL), lambda i: (0, 0)), pl.BlockSpec((1, D_MODEL), lambda i: (0, 0))],
        out_specs=pl.BlockSpec((tm, D_MODEL), lambda i: (i, 0)),
        out_shape=jax.ShapeDtypeStruct((n, D_MODEL), F32),
        compiler_params=_cparams(("parallel",)),
        name="matmul_norm_residual",
    )(x, a, w, gam.reshape(1, D_MODEL))


def _mlp_kernel(x_ref, gpre_ref, w1_ref, w2_ref, gpost_ref, o_ref, h_ref, acc_ref):
    k = pl.program_id(1)

    @pl.when(k == 0)
    def _():
        h_ref[...] = (_rms_scale(x_ref[...]) * gpre_ref[...]).astype(BF16)
        acc_ref[...] = jnp.zeros(acc_ref.shape, F32)

    f = jnp.maximum(jnp.dot(h_ref[...], w1_ref[...], preferred_element_type=F32), 0.0)
    acc_ref[...] += _dot(f * f, w2_ref[...])

    @pl.when(k == pl.num_programs(1) - 1)
    def _():
        o_ref[...] = x_ref[...] + _rms_scale(acc_ref[...]) * gpost_ref[...]


def mlp(x, gpre, w1, w2, gpost, *, tm, tk):
    n = x.shape[0]
    return pl.pallas_call(
        _mlp_kernel,
        grid=(n // tm, D_FF // tk),
        in_specs=[pl.BlockSpec((tm, D_MODEL), lambda i, k: (i, 0)), pl.BlockSpec((1, D_MODEL), lambda i, k: (0, 0)),
                  pl.BlockSpec((D_MODEL, tk), lambda i, k: (0, k)), pl.BlockSpec((tk, D_MODEL), lambda i, k: (k, 0)),
                  pl.BlockSpec((1, D_MODEL), lambda i, k: (0, 0))],
        out_specs=pl.BlockSpec((tm, D_MODEL), lambda i, k: (i, 0)),
        out_shape=jax.ShapeDtypeStruct((n, D_MODEL), F32),
        scratch_shapes=[pltpu.VMEM((tm, D_MODEL), BF16), pltpu.VMEM((tm, D_MODEL), F32)],
        compiler_params=_cparams(("parallel", "arbitrary")),
        name="mlp",
    )(x, gpre.reshape(1, D_MODEL), w1, w2, gpost.reshape(1, D_MODEL))


def reorder_w_in(w_in):
    idx = []
    acc = 0
    for s in IN_SIZES:
        idx.append((acc, acc + s))
        acc += s
    dn_qkv, dn_z, dn_b, dn_a, dil_qkv, ssm_z, ssm_xbc, ssm_dt, gates = [w_in[:, a:b] for a, b in idx]
    small = jnp.concatenate([dn_b, dn_a, ssm_dt], axis=1)
    pad = jnp.zeros((w_in.shape[0], NP - OFF_SMALL - small.shape[1]), w_in.dtype)
    return jnp.concatenate([dn_qkv, dn_z, ssm_xbc, ssm_z, gates, dil_qkv, small, pad], axis=1).astype(BF16)


def trunk_layer(x, rope, mem_k, mem_v, mem_cols, dn_conv, dn_state, ssm_conv, ssm_state, win, p, cfg):
    b, l, _ = x.shape
    n = b * l
    tm = cfg["tm"]
    xf = x.reshape(n, D_MODEL)
    proj = norm_matmul(xf, p["norm_mix_pre"], p["w_in"], tm=tm, tn=1024,
                       rope=rope + (OFF_DILQ // 1024, OFF_DILV // 1024))
    proj3 = proj.reshape(b, l, NP)

    o_dn, dn_state_new = deltanet(proj3, dn_conv, dn_state, p["dn_conv_w"], p["dn_a_log"], p["dn_dt_bias"],
                                  p["dn_norm"], rows_in=cfg["rows_in"], rows=cfg["rows"])
    o_ssm, ssm_state_new = ssd(proj3, ssm_conv, ssm_state, p["ssm_conv_w"], p["ssm_conv_b"], p["ssm_a_log"],
                               p["ssm_dt_bias"], p["ssm_d"], p["ssm_norm"], rows_in=cfg["rows_in"], rows=cfg["rows"])
    win_new = []
    if win is None:
        outs, lses = [], []
        for gi, (window, dil) in enumerate(DIL_GROUPS):
            o, lse = dilated_prompt_group(proj3, gi, dil)
            outs.append(o.reshape(n, DIL_OUT))
            lses.append(lse.reshape(n, DIL_OUT))
            keep = min(window, l)
            for off in (OFF_DILK, OFF_DILV):
                c0 = off + gi * DIL_OUT
                win_new.append(proj3[:, l - keep:, c0:c0 + DIL_OUT].reshape(b, keep, DIL_HPG, HEAD))
        dil_parts = outs + lses
    else:
        dil_parts = [dilated_sample(proj3, win).reshape(n, DIL_OUT)]
        for gi in range(len(DIL_GROUPS)):
            for off in (OFF_DILK, OFF_DILV):
                c0 = off + gi * DIL_OUT
                win_new.append(proj3[:, :, c0:c0 + DIL_OUT].reshape(b, l, DIL_HPG, HEAD))

    xf = branch_merge(xf, o_dn.reshape(n, DN_WIDTH), o_ssm.reshape(n, SSM_INNER), dil_parts, proj,
                      p["w_br_dn"], p["w_br_dil"], p["w_br_ssm"], p["w_out"], p["norm_mix_post"], tm=cfg["tm_merge"])

    qm = norm_matmul(xf, p["norm_mem_pre"], p["w_mq"], tm=tm, tn=MEM_WIDTH)
    om = mem_attention(qm.reshape(b, l, MEM_WIDTH), mem_k, mem_v, mem_cols[0], mem_cols[1], tq=cfg["tq"])
    xf = matmul_norm_residual(xf, om.reshape(n, MEM_WIDTH), p["w_mo"], p["norm_mem_post"], tm=tm)

    xf = mlp(xf, p["norm_ffn_pre"], p["w_ff1"], p["w_ff2"], p["norm_ffn_post"], tm=tm, tk=1024)

    def conv_tail(buf, off, width):
        full = jnp.concatenate([buf, proj3[:, :, off:off + width]], axis=1) if l < CONV_W - 1 else proj3[:, :, off:off + width]
        return full[:, full.shape[1] - (CONV_W - 1):]

    states = (conv_tail(dn_conv, OFF_DNQKV, 3 * DN_WIDTH), dn_state_new,
              conv_tail(ssm_conv, OFF_XBC, SSM_CONV_DIM), ssm_state_new) + tuple(win_new)
    return xf.reshape(b, l, D_MODEL), states


def kernel(x_prompt, x_sample, state_dn_conv, state_dn, state_ssm_conv, state_ssm, cache_win1_k, cache_win1_v, cache_win2_k, cache_win2_v, cache_win3_k, cache_win3_v, cache_mem_k, cache_mem_v, mem_prompt, norm_mix_pre, w_in, dn_conv_w, dn_a_log, dn_dt_bias, dn_norm, ssm_conv_w, ssm_conv_b, ssm_a_log, ssm_dt_bias, ssm_d, ssm_norm, w_br_dn, w_br_dil, w_br_ssm, w_out, norm_mix_post, norm_mem_pre, norm_mem_kv, w_mq, w_mkv, w_mo, norm_mem_post, norm_ffn_pre, w_ff1, w_ff2, norm_ffn_post):
    n_p, s = x_prompt.shape[:2]
    n_s, t = x_sample.shape[:2]
    n_mem = mem_prompt.shape[1]
    depth = w_in.shape[0]
    tm_p, tm_s = 1024, n_s * t
    rope_p = rope_tables(jnp.arange(s, dtype=F32))
    rope_s = rope_tables(jnp.tile(PAST_LEN + jnp.arange(t, dtype=F32), tm_s // t))
    cfg_p = dict(tm=tm_p, tm_merge=256, rows_in=64, rows=64, tq=512)
    cfg_s = dict(tm=tm_s, tm_merge=tm_s, rows_in=t, rows=8, tq=t)
    zero_dn_conv = jnp.zeros((n_p, CONV_W - 1, 3 * DN_WIDTH), F32)
    zero_dn = jnp.zeros((n_p, DN_HEADS, DN_DK, DN_DK), F32)
    zero_ssm_conv = jnp.zeros((n_p, CONV_W - 1, SSM_CONV_DIM), F32)
    zero_ssm = jnp.zeros((n_p, SSM_HEADS, SSM_P, SSM_N), F32)
    xp, xs = x_prompt, x_sample
    new_p = [[] for _ in range(12)]
    new_s = [[] for _ in range(10)]
    for l in range(depth):
        bf = lambda a: a[l].astype(BF16)
        prm = dict(norm_mix_pre=norm_mix_pre[l], w_in=reorder_w_in(w_in[l]), dn_conv_w=dn_conv_w[l],
                   dn_a_log=dn_a_log[l], dn_dt_bias=dn_dt_bias[l], dn_norm=dn_norm[l], ssm_conv_w=ssm_conv_w[l],
                   ssm_conv_b=ssm_conv_b[l], ssm_a_log=ssm_a_log[l], ssm_dt_bias=ssm_dt_bias[l], ssm_d=ssm_d[l],
                   ssm_norm=ssm_norm[l], w_br_dn=bf(w_br_dn), w_br_dil=bf(w_br_dil), w_br_ssm=bf(w_br_ssm),
                   w_out=bf(w_out), norm_mix_post=norm_mix_post[l], norm_mem_pre=norm_mem_pre[l], w_mq=bf(w_mq),
                   w_mo=bf(w_mo), norm_mem_post=norm_mem_post[l], norm_ffn_pre=norm_ffn_pre[l], w_ff1=bf(w_ff1),
                   w_ff2=bf(w_ff2), norm_ffn_post=norm_ffn_post[l])
        mkv = norm_matmul(mem_prompt.reshape(n_p * n_mem, D_MODEL), norm_mem_kv[l], bf(w_mkv),
                          tm=min(1024, n_p * n_mem), tn=1024)
        mkv = mkv.reshape(n_p, n_mem, 2 * MEM_WIDTH)
        mk = mkv[:, :, :MEM_WIDTH].reshape(n_p, n_mem, MEM_HEADS, HEAD)
        mv = mkv[:, :, MEM_WIDTH:].reshape(n_p, n_mem, MEM_HEADS, HEAD)
        xp, st_p = trunk_layer(xp, rope_p, mkv, mkv, (0, 1), zero_dn_conv, zero_dn, zero_ssm_conv, zero_ssm, None,
                               prm, cfg_p)
        for i, a in enumerate(st_p + (mk, mv)):
            new_p[i].append(a)
        win_l = (cache_win1_k[l], cache_win1_v[l], cache_win2_k[l], cache_win2_v[l], cache_win3_k[l], cache_win3_v[l])
        xs, st_s = trunk_layer(xs, rope_s, cache_mem_k[l].reshape(n_s, n_mem, MEM_WIDTH),
                               cache_mem_v[l].reshape(n_s, n_mem, MEM_WIDTH), (0, 0), state_dn_conv[l], state_dn[l],
                               state_ssm_conv[l], state_ssm[l], win_l, prm, cfg_s)
        for i, a in enumerate(st_s):
            new_s[i].append(a)
    outs_p = [jnp.stack(a) for a in new_p]
    outs_s = [jnp.stack(a) for a in new_s]
    return (xp, xs, *outs_p, *outs_s)
```

```python
import functools
import math

import jax
import jax.numpy as jnp
from jax import lax
from jax.experimental import pallas as pl
from jax.experimental.pallas import tpu as pltpu

F32 = jnp.float32
BF16 = jnp.bfloat16

D_MODEL = 1024
CONV_W = 4
DN_HEADS = 8
DN_DK = 128
DN_WIDTH = DN_HEADS * DN_DK
DIL_GROUPS = ((128, 1), (512, 4), (2048, 16))
DIL_HPG = 4
HEAD = 128
DIL_HEADS = DIL_HPG * len(DIL_GROUPS)
DIL_WIDTH = DIL_HEADS * HEAD
DIL_OUT = DIL_HPG * HEAD
ROT_DIM = HEAD // 4
ROPE_THETA = 500000.0
PAST_LEN = 2048
SSM_HEADS = 16
SSM_P = 64
SSM_N = 128
SSM_GROUPS = 4
SSM_HPG = SSM_HEADS // SSM_GROUPS
SSM_INNER = SSM_HEADS * SSM_P
SSM_CONV_DIM = SSM_INNER + 2 * SSM_GROUPS * SSM_N
MEM_HEADS = 4
MEM_WIDTH = MEM_HEADS * HEAD
D_FF = 4 * D_MODEL
EPS = 1e-6
NEG_INF = -1e30
IN_SIZES = (3 * DN_WIDTH, DN_WIDTH, DN_HEADS, DN_HEADS, 3 * DIL_WIDTH, SSM_INNER, SSM_CONV_DIM, SSM_HEADS,
            3 * D_MODEL)

OFF_DNQKV = 0
OFF_DNZ = OFF_DNQKV + 3 * DN_WIDTH
OFF_XBC = OFF_DNZ + DN_WIDTH
OFF_SSMZ = OFF_XBC + SSM_CONV_DIM
OFF_GATES = OFF_SSMZ + SSM_INNER
OFF_DILQ = OFF_GATES + 3 * D_MODEL
OFF_DILK = OFF_DILQ + DIL_WIDTH
OFF_DILV = OFF_DILK + DIL_WIDTH
OFF_SMALL = OFF_DILV + DIL_WIDTH
SMALL_W = 128
NP = 15 * 1024
LANE_DN_B = 0
LANE_DN_A = DN_HEADS
LANE_SSM_DT = 2 * DN_HEADS

SOLVE_BASE = 16
ATTN_UNROLL = 4

VMEM_LIMIT = 56 * 1024 * 1024


def _cparams(sem):
    return pltpu.CompilerParams(dimension_semantics=sem, vmem_limit_bytes=VMEM_LIMIT)


def _dot(a, b):
    return jnp.dot(a.astype(BF16), b.astype(BF16), preferred_element_type=F32)


def _dot_nt(a, b):
    return lax.dot_general(a.astype(BF16), b.astype(BF16), (((1,), (1,)), ((), ())), preferred_element_type=F32)


def _dot_tn(a, b):
    return lax.dot_general(a.astype(BF16), b.astype(BF16), (((0,), (0,)), ((), ())), preferred_element_type=F32)


def _split3(x):
    hi = x.astype(BF16)
    r = x - hi.astype(F32)
    mid = r.astype(BF16)
    lo = (r - mid.astype(F32)).astype(BF16)
    return hi, mid, lo


def _sel_dot(sel, x):
    return sum(jnp.dot(sel, p, preferred_element_type=F32) for p in _split3(x))


def _sel_dot_nt(sel, x):
    return sum(lax.dot_general(sel, p, (((1,), (1,)), ((), ())), preferred_element_type=F32) for p in _split3(x))


def _rms_scale(x):
    return x * lax.rsqrt(jnp.mean(x * x, axis=-1, keepdims=True) + EPS)


def _softplus(x):
    return jnp.maximum(x, 0.0) + jnp.log1p(jnp.exp(-jnp.abs(x)))


def _sigmoid(x):
    return 1.0 / (1.0 + jnp.exp(-x))


def _silu(x):
    return x * _sigmoid(x)


def _iota2(shape, axis):
    return lax.broadcasted_iota(jnp.int32, shape, axis)


def _norm_matmul_kernel(*refs, rope_tiles, tn):
    if rope_tiles is None:
        x_ref, g_ref, w_ref, o_ref, h_ref = refs
    else:
        x_ref, g_ref, w_ref, cos_ref, sa_ref, sb_ref, o_ref, h_ref = refs
    j = pl.program_id(1)

    @pl.when(j == 0)
    def _():
        h_ref[...] = (_rms_scale(x_ref[...]) * g_ref[...]).astype(BF16)

    y = jnp.dot(h_ref[...], w_ref[...], preferred_element_type=F32)
    if rope_tiles is None:
        o_ref[...] = y
        return
    lo, hi = rope_tiles
    is_rope = jnp.logical_and(j >= lo, j < hi)

    @pl.when(is_rope)
    def _():
        cos, sa, sb = cos_ref[...], sa_ref[...], sb_ref[...]
        for c in range(tn // HEAD):
            t = y[:, c * HEAD:(c + 1) * HEAD]
            o_ref[:, c * HEAD:(c + 1) * HEAD] = (
                t * cos + pltpu.roll(t, HEAD - ROT_DIM // 2, 1) * sa + pltpu.roll(t, ROT_DIM // 2, 1) * sb)

    @pl.when(jnp.logical_not(is_rope))
    def _():
        o_ref[...] = y


def norm_matmul(x, g, w, *, tm, tn, rope=None):
    n, k = x.shape
    m = w.shape[1]
    assert n % tm == 0 and m % tn == 0
    in_specs = [pl.BlockSpec((tm, k), lambda i, j: (i, 0)),
                pl.BlockSpec((1, k), lambda i, j: (0, 0)),
                pl.BlockSpec((k, tn), lambda i, j: (0, j))]
    args = [x, g.reshape(1, k), w]
    rope_tiles = None
    if rope is not None:
        cos, sa, sb, lo, hi = rope
        period = cos.shape[0] // tm
        assert cos.shape[0] % tm == 0
        tab = pl.BlockSpec((tm, HEAD), lambda i, j: (i % period, 0))
        in_specs += [tab, tab, tab]
        args += [cos, sa, sb]
        rope_tiles = (lo, hi)
    return pl.pallas_call(
        functools.partial(_norm_matmul_kernel, rope_tiles=rope_tiles, tn=tn),
        grid=(n // tm, m // tn),
        in_specs=in_specs,
        out_specs=pl.BlockSpec((tm, tn), lambda i, j: (i, j)),
        out_shape=jax.ShapeDtypeStruct((n, m), F32),
        scratch_shapes=[pltpu.VMEM((tm, k), BF16)],
        compiler_params=_cparams(("parallel", "arbitrary")),
        name="norm_matmul",
    )(*args)


def rope_tables(pos):
    half = ROT_DIM // 2
    inv = jnp.power(ROPE_THETA, -jnp.arange(half, dtype=F32) * 2.0 / ROT_DIM)
    ang = pos[:, None] * inv[None, :]
    cos, sin = jnp.cos(ang), jnp.sin(ang)
    n = pos.shape[0]
    ones = jnp.ones((n, HEAD - ROT_DIM), F32)
    zeros = jnp.zeros((n, HEAD - ROT_DIM), F32)
    z16 = jnp.zeros((n, half), F32)
    return (jnp.concatenate([cos, cos, ones], axis=1),
            jnp.concatenate([-sin, z16, zeros], axis=1),
            jnp.concatenate([z16, sin, zeros], axis=1))


def _conv_cols(xbuf, cw_ref, c0, width, rows, bias_ref=None):
    acc = xbuf[5:5 + rows, c0:c0 + width] * cw_ref[0:1, c0:c0 + width]
    for j in range(1, CONV_W):
        acc = acc + xbuf[5 + j:5 + j + rows, c0:c0 + width] * cw_ref[j:j + 1, c0:c0 + width]
    if bias_ref is not None:
        acc = acc + bias_ref[0:1, c0:c0 + width]
    return _silu(acc)


def _stage_rows(xbuf, smbuf, parts, sm_ref, cs_ref, rows_in, rows):
    @pl.when(pl.program_id(1) == 0)
    def _():
        xbuf[5:8, :] = jnp.zeros((CONV_W - 1, xbuf.shape[1]), F32) if cs_ref is None else cs_ref[...]

    @pl.when(pl.program_id(1) > 0)
    def _():
        xbuf[5:8, :] = xbuf[5 + rows_in:8 + rows_in, :]

    for ref, c0 in parts:
        xbuf[8:8 + rows_in, c0:c0 + ref.shape[1]] = ref[...]
    if rows_in < rows:
        xbuf[8 + rows_in:8 + rows, :] = jnp.zeros((rows - rows_in, xbuf.shape[1]), F32)
        smbuf[...] = jnp.zeros(smbuf.shape, F32)
        smbuf[0:rows_in, :] = sm_ref[...]
        return smbuf[...]
    return sm_ref[...]


def _lane_vec(v, lane0):
    return jnp.zeros((1, SMALL_W), F32).at[0, lane0:lane0 + v.shape[0]].set(v)


def _state_specs(layer, conv_state, state):
    cs = pl.BlockSpec((None, None) + conv_state.shape[2:], lambda i, j: (layer, i, 0, 0))
    st = pl.BlockSpec((None, None) + state.shape[2:], lambda i, j: (layer, i, 0, 0, 0))
    return cs, st


def _dn_kernel(*refs, rows_in, rows, zero_init):
    if zero_init:
        q_ref, k_ref, v_ref, z_ref, sm_ref, cw_ref, alog_ref, dtb_ref, gam_ref, o_ref, st_ref, xbuf, smbuf = refs
        cs_ref = s0_ref = None
    else:
        (q_ref, k_ref, v_ref, z_ref, sm_ref, cw_ref, alog_ref, dtb_ref, gam_ref, cs_ref, s0_ref,
         o_ref, st_ref, xbuf, smbuf) = refs
    c = rows
    heads = range(DN_HEADS)

    @pl.when(pl.program_id(1) == 0)
    def _():
        st_ref[...] = jnp.zeros(st_ref.shape, F32) if zero_init else s0_ref[...]

    sm = _stage_rows(xbuf, smbuf, ((q_ref, 0), (k_ref, DN_WIDTH), (v_ref, 2 * DN_WIDTH)), sm_ref, cs_ref,
                     rows_in, rows)
    beta_all = _sigmoid(sm)
    g_all = -jnp.exp(alog_ref[...]) * _softplus(sm + dtb_ref[...])
    if rows_in < rows:
        live = (_iota2((c, 1), 0) < rows_in).astype(F32)
        beta_all = beta_all * live
        g_all = g_all * live
    ii = _iota2((c, c), 0)
    jj = _iota2((c, c), 1)
    tri = ii >= jj
    strict = ii > jj
    eye = (ii == jj).astype(F32)
    gcum_all = _sel_dot(tri.astype(BF16), g_all)
    sel = (_iota2((DN_HEADS, SMALL_W), 1) == _iota2((DN_HEADS, SMALL_W), 0) + LANE_DN_A).astype(BF16)
    grow_all = _sel_dot_nt(sel, gcum_all)
    base = min(c, SOLVE_BASE)
    n_sq = int(math.log2(base)) - 1
    n_lvl = int(math.log2(c // base))
    assert 2 ** (n_sq + 1) == base and base * 2 ** n_lvl == c
    same_blk = [(ii >> (n_sq + 1 + lvl)) == (jj >> (n_sq + 1 + lvl)) for lvl in range(n_lvl + 1)]

    q = [_conv_cols(xbuf, cw_ref, h * DN_DK, DN_DK, c) for h in heads]
    k = [_conv_cols(xbuf, cw_ref, DN_WIDTH + h * DN_DK, DN_DK, c) for h in heads]
    v = [_conv_cols(xbuf, cw_ref, 2 * DN_WIDTH + h * DN_DK, DN_DK, c) for h in heads]
    q = [x * lax.rsqrt(jnp.sum(x * x, axis=-1, keepdims=True) + EPS) * (DN_DK ** -0.5) for x in q]
    k = [x * lax.rsqrt(jnp.sum(x * x, axis=-1, keepdims=True) + EPS) for x in k]
    gc = [gcum_all[:, LANE_DN_A + h:LANE_DN_A + h + 1] for h in heads]
    gr = [grow_all[h:h + 1, :] for h in heads]
    beta = [beta_all[:, LANE_DN_B + h:LANE_DN_B + h + 1] for h in heads]
    decay = [jnp.where(tri, jnp.exp(jnp.where(tri, gc[h] - gr[h], 0.0)), 0.0) for h in heads]
    eg = [jnp.exp(gc[h]) for h in heads]
    kb = [k[h] * beta[h] for h in heads]
    a_mat = [_dot_nt(kb[h], k[h]) * jnp.where(strict, decay[h], 0.0) for h in heads]
    qk = [_dot_nt(q[h], k[h]) * decay[h] for h in heads]
    x = [-jnp.where(same_blk[0], a_mat[h], 0.0) for h in heads]
    t_inv = [eye + x[h] for h in heads]
    for _ in range(n_sq):
        x = [_dot(x[h], x[h]) for h in heads]
        t_inv = [t_inv[h] + _dot(t_inv[h], x[h]) for h in heads]
    for lvl in range(1, len(same_blk)):
        ring = jnp.logical_and(same_blk[lvl], jnp.logical_not(same_blk[lvl - 1]))
        left = [_dot(t_inv[h], jnp.where(ring, a_mat[h], 0.0)) for h in heads]
        t_inv = [t_inv[h] - _dot(left[h], t_inv[h]) for h in heads]
    sol = [_dot(t_inv[h], jnp.concatenate([v[h] * beta[h], kb[h] * eg[h]], axis=1)) for h in heads]
    s_prev = [st_ref[h] for h in heads]
    v_new = [sol[h][:, :DN_DK] - _dot(sol[h][:, DN_DK:], s_prev[h]) for h in heads]
    o = [_dot(q[h] * eg[h], s_prev[h]) + _dot(qk[h], v_new[h]) for h in heads]
    g_last = [gc[h][c - 1:c, :] for h in heads]
    k_dec = [k[h] * jnp.exp(g_last[h] - gc[h]) for h in heads]
    s_new = [s_prev[h] * jnp.exp(g_last[h]) + _dot_tn(k_dec[h], v_new[h]) for h in heads]
    for h in heads:
        st_ref[h] = s_new[h]
        z = z_ref[:, h * DN_DK:(h + 1) * DN_DK]
        o_ref[:, h * DN_DK:(h + 1) * DN_DK] = _rms_scale(o[h][0:rows_in, :]) * gam_ref[...] * _silu(z)


def deltanet(proj, layer, conv_state, state, conv_w, a_log, dt_bias, norm_g, *, rows_in, rows):
    b, l, _ = proj.shape
    assert l % rows_in == 0 and rows_in <= rows
    zero_init = state is None
    blk = lambda w, off: pl.BlockSpec((None, rows_in, w), lambda i, j: (i, j, off // w))
    const = lambda shape: pl.BlockSpec(shape, lambda i, j: (0,) * len(shape))
    in_specs = [blk(DN_WIDTH, OFF_DNQKV), blk(DN_WIDTH, OFF_DNQKV + DN_WIDTH), blk(DN_WIDTH, OFF_DNQKV + 2 * DN_WIDTH),
                blk(DN_WIDTH, OFF_DNZ), blk(SMALL_W, OFF_SMALL),
                const((CONV_W, 3 * DN_WIDTH)), const((1, SMALL_W)), const((1, SMALL_W)), const((1, DN_DK))]
    args = [proj, proj, proj, proj, proj, conv_w, _lane_vec(a_log, LANE_DN_A), _lane_vec(dt_bias, LANE_DN_A),
            norm_g.reshape(1, DN_DK)]
    if not zero_init:
        in_specs += list(_state_specs(layer, conv_state, state))
        args += [conv_state, state]
    return pl.pallas_call(
        functools.partial(_dn_kernel, rows_in=rows_in, rows=rows, zero_init=zero_init),
        grid=(b, l // rows_in),
        in_specs=in_specs,
        out_specs=[pl.BlockSpec((None, rows_in, DN_WIDTH), lambda i, j: (i, j, 0)),
                   pl.BlockSpec((None, DN_HEADS, DN_DK, DN_DK), lambda i, j: (i, 0, 0, 0))],
        out_shape=[jax.ShapeDtypeStruct((b, l, DN_WIDTH), F32),
                   jax.ShapeDtypeStruct((b, DN_HEADS, DN_DK, DN_DK), F32)],
        scratch_shapes=[pltpu.VMEM((8 + rows, 3 * DN_WIDTH), F32), pltpu.VMEM((rows, SMALL_W), F32)],
        compiler_params=_cparams(("parallel", "arbitrary")),
        name="deltanet",
    )(*args)


def _ssd_kernel(*refs, rows_in, rows, zero_init):
    if zero_init:
        x_ref, z_ref, sm_ref, cw_ref, cb_ref, alog_ref, dtb_ref, d_ref, gam_ref, o_ref, st_ref, xbuf, smbuf = refs
        cs_ref = h0_ref = None
    else:
        (x_ref, z_ref, sm_ref, cw_ref, cb_ref, alog_ref, dtb_ref, d_ref, gam_ref, cs_ref, h0_ref,
         o_ref, st_ref, xbuf, smbuf) = refs
    c = rows
    groups = range(SSM_GROUPS)
    heads = range(SSM_HEADS)

    @pl.when(pl.program_id(1) == 0)
    def _():
        st_ref[...] = jnp.zeros(st_ref.shape, F32) if zero_init else h0_ref[...]

    sm = _stage_rows(xbuf, smbuf, ((x_ref, 0),), sm_ref, cs_ref, rows_in, rows)
    dt_all = _softplus(sm + dtb_ref[...])
    if rows_in < rows:
        dt_all = dt_all * (_iota2((c, 1), 0) < rows_in).astype(F32)
    da_all = dt_all * (-jnp.exp(alog_ref[...]))
    ii = _iota2((c, c), 0)
    jj = _iota2((c, c), 1)
    tri = ii >= jj
    acum_all = _sel_dot(tri.astype(BF16), da_all)
    sel = (_iota2((SSM_HEADS, SMALL_W), 1) == _iota2((SSM_HEADS, SMALL_W), 0) + LANE_SSM_DT).astype(BF16)
    arow_all = _sel_dot_nt(sel, acum_all)
    dtrow_all = _sel_dot_nt(sel, dt_all)
    gn = SSM_GROUPS * SSM_N
    gw = SSM_HPG * SSM_P

    bm = [_conv_cols(xbuf, cw_ref, SSM_INNER + g * SSM_N, SSM_N, c, cb_ref) for g in groups]
    cm = [_conv_cols(xbuf, cw_ref, SSM_INNER + gn + g * SSM_N, SSM_N, c, cb_ref) for g in groups]
    xg = [_conv_cols(xbuf, cw_ref, g * gw, gw, c, cb_ref) for g in groups]
    cb = [_dot_nt(cm[g], bm[g]) for g in groups]
    grp = [h // SSM_HPG for h in heads]
    xh = [xg[grp[h]][:, (h % SSM_HPG) * SSM_P:(h % SSM_HPG + 1) * SSM_P] for h in heads]
    ac = [acum_all[:, LANE_SSM_DT + h:LANE_SSM_DT + h + 1] for h in heads]
    dc = [dt_all[:, LANE_SSM_DT + h:LANE_SSM_DT + h + 1] for h in heads]
    lmat = [jnp.where(tri, jnp.exp(jnp.where(tri, ac[h] - arow_all[h:h + 1, :], 0.0)), 0.0) for h in heads]
    h_prev = [st_ref[h] for h in heads]
    y = [_dot(cb[grp[h]] * lmat[h] * dtrow_all[h:h + 1, :], xh[h])
         + _dot_nt(cm[grp[h]], h_prev[h]) * jnp.exp(ac[h]) + d_ref[0:1, h:h + 1] * xh[h] for h in heads]
    a_last = [ac[h][c - 1:c, :] for h in heads]
    wdec = [dc[h] * jnp.exp(a_last[h] - ac[h]) for h in heads]
    h_new = [h_prev[h] * jnp.exp(a_last[h]) + _dot_tn(xh[h] * wdec[h], bm[grp[h]]) for h in heads]
    for h in heads:
        st_ref[h] = h_new[h]
    for g in groups:
        yg = jnp.concatenate(y[g * SSM_HPG:(g + 1) * SSM_HPG], axis=1)
        z = z_ref[:, g * gw:(g + 1) * gw]
        o_ref[:, g * gw:(g + 1) * gw] = _rms_scale(yg[0:rows_in, :] * _silu(z)) * gam_ref[0:1, g * gw:(g + 1) * gw]


def ssd(proj, layer, conv_state, state, conv_w, conv_b, a_log, dt_bias, d_skip, norm_g, *, rows_in, rows):
    b, l, _ = proj.shape
    zero_init = state is None
    blk = lambda w, off: pl.BlockSpec((None, rows_in, w), lambda i, j: (i, j, off // w))
    const = lambda shape: pl.BlockSpec(shape, lambda i, j: (0,) * len(shape))
    in_specs = [blk(SSM_CONV_DIM, OFF_XBC), blk(SSM_INNER, OFF_SSMZ), blk(SMALL_W, OFF_SMALL),
                const((CONV_W, SSM_CONV_DIM)), const((1, SSM_CONV_DIM)), const((1, SMALL_W)), const((1, SMALL_W)),
                const((1, SMALL_W)), const((1, SSM_INNER))]
    args = [proj, proj, proj, conv_w, conv_b.reshape(1, SSM_CONV_DIM), _lane_vec(a_log, LANE_SSM_DT),
            _lane_vec(dt_bias, LANE_SSM_DT), _lane_vec(d_skip, 0), norm_g.reshape(1, SSM_INNER)]
    if not zero_init:
        in_specs += list(_state_specs(layer, conv_state, state))
        args += [conv_state, state]
    return pl.pallas_call(
        functools.partial(_ssd_kernel, rows_in=rows_in, rows=rows, zero_init=zero_init),
        grid=(b, l // rows_in),
        in_specs=in_specs,
        out_specs=[pl.BlockSpec((None, rows_in, SSM_INNER), lambda i, j: (i, j, 0)),
                   pl.BlockSpec((None, SSM_HEADS, SSM_P, SSM_N), lambda i, j: (i, 0, 0, 0))],
        out_shape=[jax.ShapeDtypeStruct((b, l, SSM_INNER), F32),
                   jax.ShapeDtypeStruct((b, SSM_HEADS, SSM_P, SSM_N), F32)],
        scratch_shapes=[pltpu.VMEM((8 + rows, SSM_CONV_DIM), F32), pltpu.VMEM((rows, SMALL_W), F32)],
        compiler_params=_cparams(("parallel", "arbitrary")),
        name="ssd",
    )(*args)


def _attend_blocks(q_ref, k_ref, v_ref, acc_ref, lse_ref, dil, blocks):
    span = HEAD
    scale = HEAD ** -0.5
    ii = _iota2((span, span), 0)
    jj = _iota2((span, span), 1)
    cur_ok = jj <= ii
    prev_ok = jj >= ii
    rows = lambda st: pl.ds(st, span) if dil == 1 else pl.ds(st, span, stride=dil)
    q = [q_ref[rows(st), :] for st, _, _ in blocks]
    s_c = [jnp.where(cur_ok, _dot_nt(q[i], k_ref[rows(st), :]) * scale, NEG_INF) for i, (st, _, _) in enumerate(blocks)]
    m = [jnp.max(s, axis=-1, keepdims=True) for s in s_c]
    s_p = []
    for i, (_, sp, ok) in enumerate(blocks):
        if sp is None:
            s_p.append(None)
            continue
        mask = prev_ok if ok is True else jnp.logical_and(prev_ok, ok)
        s_p.append(jnp.where(mask, _dot_nt(q[i], k_ref[rows(sp), :]) * scale, NEG_INF))
        m[i] = jnp.maximum(m[i], jnp.max(s_p[i], axis=-1, keepdims=True))
    for i, (st, sp, _) in enumerate(blocks):
        p_c = jnp.exp(s_c[i] - m[i])
        l = jnp.sum(p_c, axis=-1, keepdims=True)
        acc = _dot(p_c, v_ref[rows(st), :])
        if sp is not None:
            p_p = jnp.exp(s_p[i] - m[i])
            l = l + jnp.sum(p_p, axis=-1, keepdims=True)
            acc = acc + _dot(p_p, v_ref[rows(sp), :])
        acc_ref[rows(st), :] = acc / l
        lse_ref[rows(st), :] = jnp.broadcast_to(m[i] + jnp.log(l), (span, HEAD))


def _dil_prompt_kernel(*refs, seq):
    n_g = len(DIL_GROUPS)
    qs, ks, vs = refs[0:n_g], refs[n_g:2 * n_g], refs[2 * n_g:3 * n_g]
    o_ref, acc_s, lse_s = refs[3 * n_g:]
    u = ATTN_UNROLL
    for g, (_, dil) in enumerate(DIL_GROUPS):
        nb = seq // dil // HEAD
        attend = functools.partial(_attend_blocks, qs[g], ks[g], vs[g], acc_s.at[g], lse_s.at[g], dil)
        if dil == 1:
            assert nb % u == 0

            def body(it, carry, attend=attend):
                blocks = []
                for j in range(u):
                    st = pl.multiple_of((it * u + j) * HEAD, HEAD)
                    prev = pl.multiple_of(jnp.maximum(st - HEAD, 0), HEAD)
                    blocks.append((st, prev, (it > 0) if j == 0 else True))
                attend(blocks)
                return carry
            lax.fori_loop(0, nb // u, body, 0)
        elif nb > 1:
            def body(r, carry, attend=attend, dil=dil, nb=nb):
                attend([(r + n * dil * HEAD, None if n == 0 else r + (n - 1) * dil * HEAD, True) for n in range(nb)])
                return carry
            lax.fori_loop(0, dil, body, 0)
        else:
            assert dil % u == 0

            def body(it, carry, attend=attend):
                attend([(it * u + j, None, True) for j in range(u)])
                return carry
            lax.fori_loop(0, dil // u, body, 0)

    rows_per = 256

    def merge(it, carry):
        rows = pl.ds(pl.multiple_of(it * rows_per, rows_per), rows_per)
        lses = [lse_s[g, rows, :] for g in range(n_g)]
        top = functools.reduce(jnp.maximum, lses)
        wts = [jnp.exp(x - top) for x in lses]
        o_ref[rows, :] = sum(w * acc_s[g, rows, :] for g, w in enumerate(wts)) / sum(wts)
        return carry
    lax.fori_loop(0, seq // rows_per, merge, 0)


def dilated_prompt(proj):
    b, s, _ = proj.shape
    for _, dil in DIL_GROUPS:
        assert s % (dil * HEAD) == 0

    def src(off, g):
        base = off // HEAD + g * DIL_HPG
        return pl.BlockSpec((None, s, HEAD), lambda i, h: (i, 0, base + h))

    n_g = len(DIL_GROUPS)
    in_specs = [src(off, g) for off in (OFF_DILQ, OFF_DILK, OFF_DILV) for g in range(n_g)]
    return pl.pallas_call(
        functools.partial(_dil_prompt_kernel, seq=s),
        grid=(b, DIL_HPG),
        in_specs=in_specs,
        out_specs=pl.BlockSpec((None, s, HEAD), lambda i, h: (i, 0, h)),
        out_shape=jax.ShapeDtypeStruct((b, s, DIL_OUT), F32),
        scratch_shapes=[pltpu.VMEM((n_g, s, HEAD), F32), pltpu.VMEM((n_g, s, HEAD), F32)],
        compiler_params=_cparams(("parallel", "parallel")),
        name="dilated_prompt",
    )(*([proj] * (3 * n_g)))


def _dil_sample_kernel(*refs, t_new):
    qs, ks, vs = refs[0:3], refs[3:6], refs[6:9]
    caches = refs[9:15]
    o_ref = refs[15]
    scale = HEAD ** -0.5
    for t in range(t_new):
        for h in range(DIL_HPG):
            outs, lses = [], []
            for g, (_, dil) in enumerate(DIL_GROUPS):
                kc, vc = caches[2 * g], caches[2 * g + 1]
                nrow = kc.shape[0]
                cols = slice(h * HEAD, (h + 1) * HEAD)
                res = t % dil
                qrow = qs[g][t:t + 1, cols]
                s_c = jnp.sum(kc[:, res, h, :] * qrow, axis=-1, keepdims=True) * scale
                first = (t - res) // dil
                if first > 0:
                    s_c = jnp.where(_iota2((nrow, 1), 0) >= first, s_c, NEG_INF)
                s_n = jnp.sum(ks[g][:, cols] * qrow, axis=-1, keepdims=True) * scale
                tn = _iota2((t_new, 1), 0)
                new_ok = functools.reduce(jnp.logical_or, [tn == t - dil * j for j in range(t // dil + 1)])
                s_n = jnp.where(new_ok, s_n, NEG_INF)
                m = jnp.maximum(jnp.max(s_c, axis=0, keepdims=True), jnp.max(s_n, axis=0, keepdims=True))
                p_c = jnp.exp(s_c - m)
                p_n = jnp.exp(s_n - m)
                l = jnp.sum(p_c, axis=0, keepdims=True) + jnp.sum(p_n, axis=0, keepdims=True)
                acc = (jnp.sum(p_c * vc[:, res, h, :], axis=0, keepdims=True)
                       + jnp.sum(p_n * vs[g][:, cols], axis=0, keepdims=True))
                outs.append(acc / l)
                lses.append(m + jnp.log(l))
            top = functools.reduce(jnp.maximum, lses)
            wts = [jnp.exp(x - top) for x in lses]
            num = sum(w * o for w, o in zip(wts, outs))
            o_ref[t:t + 1, h * HEAD:(h + 1) * HEAD] = num / sum(wts)


def dilated_sample(proj, layer, caches):
    b, t_new, _ = proj.shape
    views, specs = [], []
    for g, (window, dil) in enumerate(DIL_GROUPS):
        for c in caches[2 * g:2 * g + 2]:
            lb = c.shape[2]
            assert lb == window and lb % dil == 0 and (dil == 1 or t_new <= dil)
            views.append(c.reshape(c.shape[0], b, lb // dil, dil, DIL_HPG, HEAD))
            specs.append(pl.BlockSpec((None, None, lb // dil, min(dil, t_new), DIL_HPG, HEAD),
                                      lambda i: (layer, i, 0, 0, 0, 0)))
    blk = lambda off, g: pl.BlockSpec((None, t_new, DIL_OUT), lambda i: (i, 0, off // DIL_OUT + g))
    in_specs = ([blk(OFF_DILQ, g) for g in range(3)] + [blk(OFF_DILK, g) for g in range(3)]
                + [blk(OFF_DILV, g) for g in range(3)] + specs)
    return pl.pallas_call(
        functools.partial(_dil_sample_kernel, t_new=t_new),
        grid=(b,),
        in_specs=in_specs,
        out_specs=pl.BlockSpec((None, t_new, DIL_OUT), lambda i: (i, 0, 0)),
        out_shape=jax.ShapeDtypeStruct((b, t_new, DIL_OUT), F32),
        compiler_params=_cparams(("parallel",)),
        name="dilated_sample",
    )(*([proj] * 9), *views)


def _merge_kernel(x_ref, odn_ref, odil_ref, ossm_ref, g_dn, g_dil, g_ssm, w_dn, w_dil, w_ssm, w_out, gam_ref, o_ref):
    merged = (_sigmoid(g_dn[...]) * _dot(odn_ref[...], w_dn[...])
              + _sigmoid(g_dil[...]) * _dot(odil_ref[...], w_dil[...])
              + _sigmoid(g_ssm[...]) * _dot(ossm_ref[...], w_ssm[...]))
    y = _dot(merged, w_out[...])
    o_ref[...] = x_ref[...] + _rms_scale(y) * gam_ref[...]


def branch_merge(x, o_dn, o_dil, o_ssm, proj, w_dn, w_dil, w_ssm, w_out, gam, *, tm):
    n = x.shape[0]
    row = lambda w: pl.BlockSpec((tm, w), lambda i: (i, 0))
    gate = lambda k: pl.BlockSpec((tm, D_MODEL), lambda i: (i, OFF_GATES // D_MODEL + k))
    const = lambda a: pl.BlockSpec(a.shape, lambda i: (0,) * a.ndim)
    gam = gam.reshape(1, D_MODEL)
    return pl.pallas_call(
        _merge_kernel,
        grid=(n // tm,),
        in_specs=[row(D_MODEL), row(DN_WIDTH), row(DIL_OUT), row(SSM_INNER), gate(0), gate(1), gate(2),
                  const(w_dn), const(w_dil), const(w_ssm), const(w_out), const(gam)],
        out_specs=row(D_MODEL),
        out_shape=jax.ShapeDtypeStruct((n, D_MODEL), F32),
        compiler_params=_cparams(("parallel",)),
        name="branch_merge",
    )(x, o_dn, o_dil, o_ssm, proj, proj, proj, w_dn, w_dil, w_ssm, w_out, gam)


def _mem_attn_kernel(q_ref, k_ref, v_ref, o_ref, qbuf, *, rows_in, rows, head_major):
    scale = HEAD ** -0.5
    if rows_in < rows:
        qbuf[...] = jnp.zeros(qbuf.shape, F32)
        qbuf[0:rows_in, :] = q_ref[...]
        q_all = qbuf
    else:
        q_all = q_ref
    for h in range(MEM_HEADS):
        cols = slice(h * HEAD, (h + 1) * HEAD)
        k = k_ref[:, h, :] if head_major else k_ref[:, cols]
        v = v_ref[:, h, :] if head_major else v_ref[:, cols]
        s = _dot_nt(q_all[:, cols], k) * scale
        m = jnp.max(s, axis=-1, keepdims=True)
        p = jnp.exp(s - m)
        o = _dot(p, v) / jnp.sum(p, axis=-1, keepdims=True)
        o_ref[:, cols] = o[0:rows_in, :]


def mem_attention(q, k, v, kv_specs, *, tq, head_major):
    b, l, _ = q.shape
    rows = max(tq, 8)
    return pl.pallas_call(
        functools.partial(_mem_attn_kernel, rows_in=tq, rows=rows, head_major=head_major),
        grid=(b, l // tq),
        in_specs=[pl.BlockSpec((None, tq, MEM_WIDTH), lambda i, j: (i, j, 0))] + list(kv_specs),
        out_specs=pl.BlockSpec((None, tq, MEM_WIDTH), lambda i, j: (i, j, 0)),
        out_shape=jax.ShapeDtypeStruct((b, l, MEM_WIDTH), F32),
        scratch_shapes=[pltpu.VMEM((rows, MEM_WIDTH), F32)],
        compiler_params=_cparams(("parallel", "parallel")),
        name="mem_attention",
    )(q, k, v)


def _matmul_norm_res_kernel(x_ref, a_ref, w_ref, gam_ref, o_ref):
    y = _dot(a_ref[...], w_ref[...])
    o_ref[...] = x_ref[...] + _rms_scale(y) * gam_ref[...]


def matmul_norm_residual(x, a, w, gam, *, tm):
    n, k = a.shape
    return pl.pallas_call(
        _matmul_norm_res_kernel,
        grid=(n // tm,),
        in_specs=[pl.BlockSpec((tm, D_MODEL), lambda i: (i, 0)), pl.BlockSpec((tm, k), lambda i: (i, 0)),
                  pl.BlockSpec((k, D_MODEL), lambda i: (0, 0)), pl.BlockSpec((1, D_MODEL), lambda i: (0, 0))],
        out_specs=pl.BlockSpec((tm, D_MODEL), lambda i: (i, 0)),
        out_shape=jax.ShapeDtypeStruct((n, D_MODEL), F32),
        compiler_params=_cparams(("parallel",)),
        name="matmul_norm_residual",
    )(x, a, w, gam.reshape(1, D_MODEL))


def _mlp_kernel(x_ref, gpre_ref, w1_ref, w2_ref, gpost_ref, o_ref, h_ref, acc_ref):
    k = pl.program_id(1)

    @pl.when(k == 0)
    def _():
        h_ref[...] = (_rms_scale(x_ref[...]) * gpre_ref[...]).astype(BF16)
        acc_ref[...] = jnp.zeros(acc_ref.shape, F32)

    f = jnp.maximum(jnp.dot(h_ref[...], w1_ref[...], preferred_element_type=F32), 0.0)
    acc_ref[...] += _dot(f * f, w2_ref[...])

    @pl.when(k == pl.num_programs(1) - 1)
    def _():
        o_ref[...] = x_ref[...] + _rms_scale(acc_ref[...]) * gpost_ref[...]


def mlp(x, gpre, w1, w2, gpost, *, tm, tk):
    n = x.shape[0]
    return pl.pallas_call(
        _mlp_kernel,
        grid=(n // tm, D_FF // tk),
        in_specs=[pl.BlockSpec((tm, D_MODEL), lambda i, k: (i, 0)), pl.BlockSpec((1, D_MODEL), lambda i, k: (0, 0)),
                  pl.BlockSpec((D_MODEL, tk), lambda i, k: (0, k)), pl.BlockSpec((tk, D_MODEL), lambda i, k: (k, 0)),
                  pl.BlockSpec((1, D_MODEL), lambda i, k: (0, 0))],
        out_specs=pl.BlockSpec((tm, D_MODEL), lambda i, k: (i, 0)),
        out_shape=jax.ShapeDtypeStruct((n, D_MODEL), F32),
        scratch_shapes=[pltpu.VMEM((tm, D_MODEL), BF16), pltpu.VMEM((tm, D_MODEL), F32)],
        compiler_params=_cparams(("parallel", "arbitrary")),
        name="mlp",
    )(x, gpre.reshape(1, D_MODEL), w1, w2, gpost.reshape(1, D_MODEL))


def reorder_w_in(w_in):
    idx = []
    acc = 0
    for s in IN_SIZES:
        idx.append((acc, acc + s))
        acc += s
    dn_qkv, dn_z, dn_b, dn_a, dil_qkv, ssm_z, ssm_xbc, ssm_dt, gates = [w_in[:, a:b] for a, b in idx]
    small = jnp.concatenate([dn_b, dn_a, ssm_dt], axis=1)
    pad = jnp.zeros((w_in.shape[0], NP - OFF_SMALL - small.shape[1]), w_in.dtype)
    return jnp.concatenate([dn_qkv, dn_z, ssm_xbc, ssm_z, gates, dil_qkv, small, pad], axis=1).astype(BF16)


def trunk_layer(x, layer, rope, mem, states, p, cfg):
    b, l, _ = x.shape
    n = b * l
    tm = cfg["tm"]
    xf = x.reshape(n, D_MODEL)
    proj = norm_matmul(xf, p["norm_mix_pre"], p["w_in"], tm=tm, tn=1024,
                       rope=rope + (OFF_DILQ // 1024, OFF_DILV // 1024))
    proj3 = proj.reshape(b, l, NP)
    dn_conv, dn_state, ssm_conv, ssm_state = states[:4] if states is not None else (None,) * 4

    o_dn, dn_state_new = deltanet(proj3, layer, dn_conv, dn_state, p["dn_conv_w"], p["dn_a_log"], p["dn_dt_bias"],
                                  p["dn_norm"], rows_in=cfg["rows_in"], rows=cfg["rows"])
    o_ssm, ssm_state_new = ssd(proj3, layer, ssm_conv, ssm_state, p["ssm_conv_w"], p["ssm_conv_b"], p["ssm_a_log"],
                               p["ssm_dt_bias"], p["ssm_d"], p["ssm_norm"], rows_in=cfg["rows_in"], rows=cfg["rows"])
    win_new = []
    o_dil = dilated_prompt(proj3) if states is None else dilated_sample(proj3, layer, states[4:])
    for gi, (window, _) in enumerate(DIL_GROUPS):
        keep = min(window, l) if states is None else l
        for off in (OFF_DILK, OFF_DILV):
            c0 = off + gi * DIL_OUT
            win_new.append(proj3[:, l - keep:, c0:c0 + DIL_OUT].reshape(b, keep, DIL_HPG, HEAD))

    xf = branch_merge(xf, o_dn.reshape(n, DN_WIDTH), o_dil.reshape(n, DIL_OUT), o_ssm.reshape(n, SSM_INNER), proj,
                      p["w_br_dn"], p["w_br_dil"], p["w_br_ssm"], p["w_out"], p["norm_mix_post"], tm=cfg["tm_merge"])

    qm = norm_matmul(xf, p["norm_mem_pre"], p["w_mq"], tm=tm, tn=MEM_WIDTH)
    mem_k, mem_v, mem_specs, head_major = mem
    om = mem_attention(qm.reshape(b, l, MEM_WIDTH), mem_k, mem_v, mem_specs, tq=cfg["tq"], head_major=head_major)
    xf = matmul_norm_residual(xf, om.reshape(n, MEM_WIDTH), p["w_mo"], p["norm_mem_post"], tm=tm)

    xf = mlp(xf, p["norm_ffn_pre"], p["w_ff1"], p["w_ff2"], p["norm_ffn_post"], tm=tm, tk=1024)

    def conv_tail(buf, off, width):
        new = proj3[:, :, off:off + width]
        if l < CONV_W - 1:
            prev = jnp.zeros((b, CONV_W - 1, width), F32) if buf is None else buf[layer]
            new = jnp.concatenate([prev, new], axis=1)
        return new[:, new.shape[1] - (CONV_W - 1):]

    new_states = (conv_tail(dn_conv, OFF_DNQKV, 3 * DN_WIDTH), dn_state_new,
                  conv_tail(ssm_conv, OFF_XBC, SSM_CONV_DIM), ssm_state_new) + tuple(win_new)
    return xf.reshape(b, l, D_MODEL), new_states


def kernel(x_prompt, x_sample, state_dn_conv, state_dn, state_ssm_conv, state_ssm, cache_win1_k, cache_win1_v, cache_win2_k, cache_win2_v, cache_win3_k, cache_win3_v, cache_mem_k, cache_mem_v, mem_prompt, norm_mix_pre, w_in, dn_conv_w, dn_a_log, dn_dt_bias, dn_norm, ssm_conv_w, ssm_conv_b, ssm_a_log, ssm_dt_bias, ssm_d, ssm_norm, w_br_dn, w_br_dil, w_br_ssm, w_out, norm_mix_post, norm_mem_pre, norm_mem_kv, w_mq, w_mkv, w_mo, norm_mem_post, norm_ffn_pre, w_ff1, w_ff2, norm_ffn_post):
    n_p, s = x_prompt.shape[:2]
    n_s, t = x_sample.shape[:2]
    n_mem = mem_prompt.shape[1]
    depth = w_in.shape[0]
    tm_p, tm_s = 1024, n_s * t
    rope_p = rope_tables(jnp.arange(s, dtype=F32))
    rope_s = rope_tables(jnp.tile(PAST_LEN + jnp.arange(t, dtype=F32), tm_s // t))
    cfg_p = dict(tm=tm_p, tm_merge=256, rows_in=64, rows=64, tq=512)
    cfg_s = dict(tm=tm_s, tm_merge=tm_s, rows_in=t, rows=8, tq=t)
    states_s = (state_dn_conv, state_dn, state_ssm_conv, state_ssm, cache_win1_k, cache_win1_v, cache_win2_k,
                cache_win2_v, cache_win3_k, cache_win3_v)
    xp, xs = x_prompt, x_sample
    new_p = [[] for _ in range(12)]
    new_s = [[] for _ in range(10)]
    for l in range(depth):
        bf = lambda a: a[l].astype(BF16)
        prm = dict(norm_mix_pre=norm_mix_pre[l], w_in=reorder_w_in(w_in[l]), dn_conv_w=dn_conv_w[l],
                   dn_a_log=dn_a_log[l], dn_dt_bias=dn_dt_bias[l], dn_norm=dn_norm[l], ssm_conv_w=ssm_conv_w[l],
                   ssm_conv_b=ssm_conv_b[l], ssm_a_log=ssm_a_log[l], ssm_dt_bias=ssm_dt_bias[l], ssm_d=ssm_d[l],
                   ssm_norm=ssm_norm[l], w_br_dn=bf(w_br_dn), w_br_dil=bf(w_br_dil), w_br_ssm=bf(w_br_ssm),
                   w_out=bf(w_out), norm_mix_post=norm_mix_post[l], norm_mem_pre=norm_mem_pre[l], w_mq=bf(w_mq),
                   w_mo=bf(w_mo), norm_mem_post=norm_mem_post[l], norm_ffn_pre=norm_ffn_pre[l], w_ff1=bf(w_ff1),
                   w_ff2=bf(w_ff2), norm_ffn_post=norm_ffn_post[l])
        mkv = norm_matmul(mem_prompt.reshape(n_p * n_mem, D_MODEL), norm_mem_kv[l], bf(w_mkv),
                          tm=min(1024, n_p * n_mem), tn=1024)
        mkv = mkv.reshape(n_p, n_mem, 2 * MEM_WIDTH)
        mk = mkv[:, :, :MEM_WIDTH].reshape(n_p, n_mem, MEM_HEADS, HEAD)
        mv = mkv[:, :, MEM_WIDTH:].reshape(n_p, n_mem, MEM_HEADS, HEAD)
        mem_p = (mkv, mkv, [pl.BlockSpec((None, n_mem, MEM_WIDTH), lambda i, j: (i, 0, 0)),
                            pl.BlockSpec((None, n_mem, MEM_WIDTH), lambda i, j: (i, 0, 1))], False)
        xp, st_p = trunk_layer(xp, l, rope_p, mem_p, None, prm, cfg_p)
        for i, a in enumerate(st_p + (mk, mv)):
            new_p[i].append(a)
        cache_spec = pl.BlockSpec((None, None, n_mem, MEM_HEADS, HEAD), lambda i, j, l=l: (l, i, 0, 0, 0))
        mem_s = (cache_mem_k, cache_mem_v, [cache_spec, cache_spec], True)
        xs, st_s = trunk_layer(xs, l, rope_s, mem_s, states_s, prm, cfg_s)
        for i, a in enumerate(st_s):
            new_s[i].append(a)
    outs_p = [jnp.stack(a) for a in new_p]
    outs_s = [jnp.stack(a) for a in new_s]
    return (xp, xs, *outs_p, *outs_s)
```

```python
import functools
import math

import jax
import jax.numpy as jnp
from jax import lax
from jax.experimental import pallas as pl
from jax.experimental.pallas import tpu as pltpu

F32 = jnp.float32
BF16 = jnp.bfloat16

D_MODEL = 1024
CONV_W = 4
DN_HEADS = 8
DN_DK = 128
DN_WIDTH = DN_HEADS * DN_DK
DIL_GROUPS = ((128, 1), (512, 4), (2048, 16))
DIL_HPG = 4
HEAD = 128
DIL_HEADS = DIL_HPG * len(DIL_GROUPS)
DIL_WIDTH = DIL_HEADS * HEAD
DIL_OUT = DIL_HPG * HEAD
ROT_DIM = HEAD // 4
ROPE_THETA = 500000.0
PAST_LEN = 2048
SSM_HEADS = 16
SSM_P = 64
SSM_N = 128
SSM_GROUPS = 4
SSM_HPG = SSM_HEADS // SSM_GROUPS
SSM_INNER = SSM_HEADS * SSM_P
SSM_CONV_DIM = SSM_INNER + 2 * SSM_GROUPS * SSM_N
MEM_HEADS = 4
MEM_WIDTH = MEM_HEADS * HEAD
D_FF = 4 * D_MODEL
EPS = 1e-6
NEG_INF = -1e30
IN_SIZES = (3 * DN_WIDTH, DN_WIDTH, DN_HEADS, DN_HEADS, 3 * DIL_WIDTH, SSM_INNER, SSM_CONV_DIM, SSM_HEADS,
            3 * D_MODEL)

OFF_DNQKV = 0
OFF_DNZ = OFF_DNQKV + 3 * DN_WIDTH
OFF_XBC = OFF_DNZ + DN_WIDTH
OFF_SSMZ = OFF_XBC + SSM_CONV_DIM
OFF_GATES = OFF_SSMZ + SSM_INNER
OFF_DILQ = OFF_GATES + 3 * D_MODEL
OFF_DILK = OFF_DILQ + DIL_WIDTH
OFF_DILV = OFF_DILK + DIL_WIDTH
OFF_SMALL = OFF_DILV + DIL_WIDTH
SMALL_W = 128
NP = 15 * 1024
LANE_DN_B = 0
LANE_DN_A = DN_HEADS
LANE_SSM_DT = 2 * DN_HEADS

SOLVE_BASE = 16
ATTN_UNROLL = 4

VMEM_LIMIT = 56 * 1024 * 1024


def _cparams(sem):
    return pltpu.CompilerParams(dimension_semantics=sem, vmem_limit_bytes=VMEM_LIMIT)


def _dot(a, b):
    return jnp.dot(a.astype(BF16), b.astype(BF16), preferred_element_type=F32)


def _dot_nt(a, b):
    return lax.dot_general(a.astype(BF16), b.astype(BF16), (((1,), (1,)), ((), ())), preferred_element_type=F32)


def _dot_tn(a, b):
    return lax.dot_general(a.astype(BF16), b.astype(BF16), (((0,), (0,)), ((), ())), preferred_element_type=F32)


def _split3(x):
    hi = x.astype(BF16)
    r = x - hi.astype(F32)
    mid = r.astype(BF16)
    lo = (r - mid.astype(F32)).astype(BF16)
    return hi, mid, lo


def _sel_dot(sel, x):
    return sum(jnp.dot(sel, p, preferred_element_type=F32) for p in _split3(x))


def _sel_dot_nt(sel, x):
    return sum(lax.dot_general(sel, p, (((1,), (1,)), ((), ())), preferred_element_type=F32) for p in _split3(x))


def _rms_scale(x):
    return x * lax.rsqrt(jnp.mean(x * x, axis=-1, keepdims=True) + EPS)


def _softplus(x):
    return jnp.maximum(x, 0.0) + jnp.log1p(jnp.exp(-jnp.abs(x)))


def _sigmoid(x):
    return 1.0 / (1.0 + jnp.exp(-x))


def _silu(x):
    return x * _sigmoid(x)


def _iota2(shape, axis):
    return lax.broadcasted_iota(jnp.int32, shape, axis)


def _norm_matmul_kernel(*refs, rope_tiles, tn):
    if rope_tiles is None:
        x_ref, g_ref, w_ref, o_ref, h_ref = refs
    else:
        x_ref, g_ref, w_ref, cos_ref, sa_ref, sb_ref, o_ref, h_ref = refs
    j = pl.program_id(1)

    @pl.when(j == 0)
    def _():
        h_ref[...] = (_rms_scale(x_ref[...]) * g_ref[...]).astype(BF16)

    y = jnp.dot(h_ref[...], w_ref[...], preferred_element_type=F32)
    if rope_tiles is None:
        o_ref[...] = y
        return
    lo, hi = rope_tiles
    is_rope = jnp.logical_and(j >= lo, j < hi)

    @pl.when(is_rope)
    def _():
        cos, sa, sb = cos_ref[...], sa_ref[...], sb_ref[...]
        for c in range(tn // HEAD):
            t = y[:, c * HEAD:(c + 1) * HEAD]
            o_ref[:, c * HEAD:(c + 1) * HEAD] = (
                t * cos + pltpu.roll(t, HEAD - ROT_DIM // 2, 1) * sa + pltpu.roll(t, ROT_DIM // 2, 1) * sb)

    @pl.when(jnp.logical_not(is_rope))
    def _():
        o_ref[...] = y


def norm_matmul(x, g, w, *, tm, tn, rope=None):
    n, k = x.shape
    m = w.shape[1]
    assert n % tm == 0 and m % tn == 0
    in_specs = [pl.BlockSpec((tm, k), lambda i, j: (i, 0)),
                pl.BlockSpec((1, k), lambda i, j: (0, 0)),
                pl.BlockSpec((k, tn), lambda i, j: (0, j))]
    args = [x, g.reshape(1, k), w]
    rope_tiles = None
    if rope is not None:
        cos, sa, sb, lo, hi = rope
        period = cos.shape[0] // tm
        assert cos.shape[0] % tm == 0
        tab = pl.BlockSpec((tm, HEAD), lambda i, j: (i % period, 0))
        in_specs += [tab, tab, tab]
        args += [cos, sa, sb]
        rope_tiles = (lo, hi)
    return pl.pallas_call(
        functools.partial(_norm_matmul_kernel, rope_tiles=rope_tiles, tn=tn),
        grid=(n // tm, m // tn),
        in_specs=in_specs,
        out_specs=pl.BlockSpec((tm, tn), lambda i, j: (i, j)),
        out_shape=jax.ShapeDtypeStruct((n, m), F32),
        scratch_shapes=[pltpu.VMEM((tm, k), BF16)],
        compiler_params=_cparams(("parallel", "arbitrary")),
        name="norm_matmul",
    )(*args)


def rope_tables(pos):
    half = ROT_DIM // 2
    inv = jnp.power(ROPE_THETA, -jnp.arange(half, dtype=F32) * 2.0 / ROT_DIM)
    ang = pos[:, None] * inv[None, :]
    cos, sin = jnp.cos(ang), jnp.sin(ang)
    n = pos.shape[0]
    ones = jnp.ones((n, HEAD - ROT_DIM), F32)
    zeros = jnp.zeros((n, HEAD - ROT_DIM), F32)
    z16 = jnp.zeros((n, half), F32)
    return (jnp.concatenate([cos, cos, ones], axis=1),
            jnp.concatenate([-sin, z16, zeros], axis=1),
            jnp.concatenate([z16, sin, zeros], axis=1))


def _conv_cols(xbuf, cw_ref, c0, width, rows, bias_ref=None):
    acc = xbuf[5:5 + rows, c0:c0 + width] * cw_ref[0:1, c0:c0 + width]
    for j in range(1, CONV_W):
        acc = acc + xbuf[5 + j:5 + j + rows, c0:c0 + width] * cw_ref[j:j + 1, c0:c0 + width]
    if bias_ref is not None:
        acc = acc + bias_ref[0:1, c0:c0 + width]
    return _silu(acc)


def _stage_rows(xbuf, smbuf, parts, sm_ref, cs_ref, rows_in, rows):
    @pl.when(pl.program_id(1) == 0)
    def _():
        xbuf[5:8, :] = jnp.zeros((CONV_W - 1, xbuf.shape[1]), F32) if cs_ref is None else cs_ref[...]

    @pl.when(pl.program_id(1) > 0)
    def _():
        xbuf[5:8, :] = xbuf[5 + rows_in:8 + rows_in, :]

    for ref, c0 in parts:
        xbuf[8:8 + rows_in, c0:c0 + ref.shape[1]] = ref[...]
    if rows_in < rows:
        xbuf[8 + rows_in:8 + rows, :] = jnp.zeros((rows - rows_in, xbuf.shape[1]), F32)
        smbuf[...] = jnp.zeros(smbuf.shape, F32)
        smbuf[0:rows_in, :] = sm_ref[...]
        return smbuf[...]
    return sm_ref[...]


def _lane_vec(v, lane0):
    return jnp.zeros((1, SMALL_W), F32).at[0, lane0:lane0 + v.shape[0]].set(v)


def _state_specs(layer, conv_state, state):
    cs = pl.BlockSpec((None, None) + conv_state.shape[2:], lambda i, j: (layer, i, 0, 0))
    st = pl.BlockSpec((None, None) + state.shape[2:], lambda i, j: (layer, i, 0, 0, 0))
    return cs, st


def _dn_kernel(*refs, rows_in, rows, zero_init):
    if zero_init:
        q_ref, k_ref, v_ref, z_ref, sm_ref, cw_ref, alog_ref, dtb_ref, gam_ref, o_ref, st_ref, xbuf, smbuf = refs
        cs_ref = s0_ref = None
    else:
        (q_ref, k_ref, v_ref, z_ref, sm_ref, cw_ref, alog_ref, dtb_ref, gam_ref, cs_ref, s0_ref,
         o_ref, st_ref, xbuf, smbuf) = refs
    c = rows
    heads = range(DN_HEADS)

    @pl.when(pl.program_id(1) == 0)
    def _():
        st_ref[...] = jnp.zeros(st_ref.shape, F32) if zero_init else s0_ref[...]

    sm = _stage_rows(xbuf, smbuf, ((q_ref, 0), (k_ref, DN_WIDTH), (v_ref, 2 * DN_WIDTH)), sm_ref, cs_ref,
                     rows_in, rows)
    beta_all = _sigmoid(sm)
    g_all = -jnp.exp(alog_ref[...]) * _softplus(sm + dtb_ref[...])
    if rows_in < rows:
        live = (_iota2((c, 1), 0) < rows_in).astype(F32)
        beta_all = beta_all * live
        g_all = g_all * live
    ii = _iota2((c, c), 0)
    jj = _iota2((c, c), 1)
    tri = ii >= jj
    strict = ii > jj
    eye = (ii == jj).astype(F32)
    gcum_all = _sel_dot(tri.astype(BF16), g_all)
    sel = (_iota2((DN_HEADS, SMALL_W), 1) == _iota2((DN_HEADS, SMALL_W), 0) + LANE_DN_A).astype(BF16)
    grow_all = _sel_dot_nt(sel, gcum_all)
    base = min(c, SOLVE_BASE)
    n_sq = int(math.log2(base)) - 1
    n_lvl = int(math.log2(c // base))
    assert 2 ** (n_sq + 1) == base and base * 2 ** n_lvl == c
    same_blk = [(ii >> (n_sq + 1 + lvl)) == (jj >> (n_sq + 1 + lvl)) for lvl in range(n_lvl + 1)]

    q = [_conv_cols(xbuf, cw_ref, h * DN_DK, DN_DK, c) for h in heads]
    k = [_conv_cols(xbuf, cw_ref, DN_WIDTH + h * DN_DK, DN_DK, c) for h in heads]
    v = [_conv_cols(xbuf, cw_ref, 2 * DN_WIDTH + h * DN_DK, DN_DK, c) for h in heads]
    q = [x * lax.rsqrt(jnp.sum(x * x, axis=-1, keepdims=True) + EPS) * (DN_DK ** -0.5) for x in q]
    k = [x * lax.rsqrt(jnp.sum(x * x, axis=-1, keepdims=True) + EPS) for x in k]
    gc = [gcum_all[:, LANE_DN_A + h:LANE_DN_A + h + 1] for h in heads]
    gr = [grow_all[h:h + 1, :] for h in heads]
    beta = [beta_all[:, LANE_DN_B + h:LANE_DN_B + h + 1] for h in heads]
    decay = [jnp.where(tri, jnp.exp(jnp.where(tri, gc[h] - gr[h], 0.0)), 0.0) for h in heads]
    eg = [jnp.exp(gc[h]) for h in heads]
    kb = [k[h] * beta[h] for h in heads]
    a_mat = [_dot_nt(kb[h], k[h]) * jnp.where(strict, decay[h], 0.0) for h in heads]
    qk = [_dot_nt(q[h], k[h]) * decay[h] for h in heads]
    x = [-jnp.where(same_blk[0], a_mat[h], 0.0) for h in heads]
    t_inv = [eye + x[h] for h in heads]
    for _ in range(n_sq):
        x = [_dot(x[h], x[h]) for h in heads]
        t_inv = [t_inv[h] + _dot(t_inv[h], x[h]) for h in heads]
    for lvl in range(1, len(same_blk)):
        ring = jnp.logical_and(same_blk[lvl], jnp.logical_not(same_blk[lvl - 1]))
        left = [_dot(t_inv[h], jnp.where(ring, a_mat[h], 0.0)) for h in heads]
        t_inv = [t_inv[h] - _dot(left[h], t_inv[h]) for h in heads]
    sol = [_dot(t_inv[h], jnp.concatenate([v[h] * beta[h], kb[h] * eg[h]], axis=1)) for h in heads]
    s_prev = [st_ref[h] for h in heads]
    v_new = [sol[h][:, :DN_DK] - _dot(sol[h][:, DN_DK:], s_prev[h]) for h in heads]
    o = [_dot(q[h] * eg[h], s_prev[h]) + _dot(qk[h], v_new[h]) for h in heads]
    g_last = [gc[h][c - 1:c, :] for h in heads]
    k_dec = [k[h] * jnp.exp(g_last[h] - gc[h]) for h in heads]
    s_new = [s_prev[h] * jnp.exp(g_last[h]) + _dot_tn(k_dec[h], v_new[h]) for h in heads]
    for h in heads:
        st_ref[h] = s_new[h]
        z = z_ref[:, h * DN_DK:(h + 1) * DN_DK]
        o_ref[:, h * DN_DK:(h + 1) * DN_DK] = _rms_scale(o[h][0:rows_in, :]) * gam_ref[...] * _silu(z)


def deltanet(proj, layer, conv_state, state, conv_w, a_log, dt_bias, norm_g, *, rows_in, rows):
    b, l, _ = proj.shape
    assert l % rows_in == 0 and rows_in <= rows
    zero_init = state is None
    blk = lambda w, off: pl.BlockSpec((None, rows_in, w), lambda i, j: (i, j, off // w))
    const = lambda shape: pl.BlockSpec(shape, lambda i, j: (0,) * len(shape))
    in_specs = [blk(DN_WIDTH, OFF_DNQKV), blk(DN_WIDTH, OFF_DNQKV + DN_WIDTH), blk(DN_WIDTH, OFF_DNQKV + 2 * DN_WIDTH),
                blk(DN_WIDTH, OFF_DNZ), blk(SMALL_W, OFF_SMALL),
                const((CONV_W, 3 * DN_WIDTH)), const((1, SMALL_W)), const((1, SMALL_W)), const((1, DN_DK))]
    args = [proj, proj, proj, proj, proj, conv_w, _lane_vec(a_log, LANE_DN_A), _lane_vec(dt_bias, LANE_DN_A),
            norm_g.reshape(1, DN_DK)]
    if not zero_init:
        in_specs += list(_state_specs(layer, conv_state, state))
        args += [conv_state, state]
    return pl.pallas_call(
        functools.partial(_dn_kernel, rows_in=rows_in, rows=rows, zero_init=zero_init),
        grid=(b, l // rows_in),
        in_specs=in_specs,
        out_specs=[pl.BlockSpec((None, rows_in, DN_WIDTH), lambda i, j: (i, j, 0)),
                   pl.BlockSpec((None, DN_HEADS, DN_DK, DN_DK), lambda i, j: (i, 0, 0, 0))],
        out_shape=[jax.ShapeDtypeStruct((b, l, DN_WIDTH), F32),
                   jax.ShapeDtypeStruct((b, DN_HEADS, DN_DK, DN_DK), F32)],
        scratch_shapes=[pltpu.VMEM((8 + rows, 3 * DN_WIDTH), F32), pltpu.VMEM((rows, SMALL_W), F32)],
        compiler_params=_cparams(("parallel", "arbitrary")),
        name="deltanet",
    )(*args)


def _ssd_kernel(*refs, rows_in, rows, zero_init):
    if zero_init:
        x_ref, z_ref, sm_ref, cw_ref, cb_ref, alog_ref, dtb_ref, d_ref, gam_ref, o_ref, st_ref, xbuf, smbuf = refs
        cs_ref = h0_ref = None
    else:
        (x_ref, z_ref, sm_ref, cw_ref, cb_ref, alog_ref, dtb_ref, d_ref, gam_ref, cs_ref, h0_ref,
         o_ref, st_ref, xbuf, smbuf) = refs
    c = rows
    groups = range(SSM_GROUPS)
    heads = range(SSM_HEADS)

    @pl.when(pl.program_id(1) == 0)
    def _():
        st_ref[...] = jnp.zeros(st_ref.shape, F32) if zero_init else h0_ref[...]

    sm = _stage_rows(xbuf, smbuf, ((x_ref, 0),), sm_ref, cs_ref, rows_in, rows)
    dt_all = _softplus(sm + dtb_ref[...])
    if rows_in < rows:
        dt_all = dt_all * (_iota2((c, 1), 0) < rows_in).astype(F32)
    da_all = dt_all * (-jnp.exp(alog_ref[...]))
    ii = _iota2((c, c), 0)
    jj = _iota2((c, c), 1)
    tri = ii >= jj
    acum_all = _sel_dot(tri.astype(BF16), da_all)
    sel = (_iota2((SSM_HEADS, SMALL_W), 1) == _iota2((SSM_HEADS, SMALL_W), 0) + LANE_SSM_DT).astype(BF16)
    arow_all = _sel_dot_nt(sel, acum_all)
    dtrow_all = _sel_dot_nt(sel, dt_all)
    gn = SSM_GROUPS * SSM_N
    gw = SSM_HPG * SSM_P

    bm = [_conv_cols(xbuf, cw_ref, SSM_INNER + g * SSM_N, SSM_N, c, cb_ref) for g in groups]
    cm = [_conv_cols(xbuf, cw_ref, SSM_INNER + gn + g * SSM_N, SSM_N, c, cb_ref) for g in groups]
    xg = [_conv_cols(xbuf, cw_ref, g * gw, gw, c, cb_ref) for g in groups]
    cb = [_dot_nt(cm[g], bm[g]) for g in groups]
    grp = [h // SSM_HPG for h in heads]
    xh = [xg[grp[h]][:, (h % SSM_HPG) * SSM_P:(h % SSM_HPG + 1) * SSM_P] for h in heads]
    ac = [acum_all[:, LANE_SSM_DT + h:LANE_SSM_DT + h + 1] for h in heads]
    dc = [dt_all[:, LANE_SSM_DT + h:LANE_SSM_DT + h + 1] for h in heads]
    lmat = [jnp.where(tri, jnp.exp(jnp.where(tri, ac[h] - arow_all[h:h + 1, :], 0.0)), 0.0) for h in heads]
    h_prev = [st_ref[h] for h in heads]
    y = [_dot(cb[grp[h]] * lmat[h] * dtrow_all[h:h + 1, :], xh[h])
         + _dot_nt(cm[grp[h]], h_prev[h]) * jnp.exp(ac[h]) + d_ref[0:1, h:h + 1] * xh[h] for h in heads]
    a_last = [ac[h][c - 1:c, :] for h in heads]
    wdec = [dc[h] * jnp.exp(a_last[h] - ac[h]) for h in heads]
    h_new = [h_prev[h] * jnp.exp(a_last[h]) + _dot_tn(xh[h] * wdec[h], bm[grp[h]]) for h in heads]
    for h in heads:
        st_ref[h] = h_new[h]
    for g in groups:
        yg = jnp.concatenate(y[g * SSM_HPG:(g + 1) * SSM_HPG], axis=1)
        z = z_ref[:, g * gw:(g + 1) * gw]
        o_ref[:, g * gw:(g + 1) * gw] = _rms_scale(yg[0:rows_in, :] * _silu(z)) * gam_ref[0:1, g * gw:(g + 1) * gw]


def ssd(proj, layer, conv_state, state, conv_w, conv_b, a_log, dt_bias, d_skip, norm_g, *, rows_in, rows):
    b, l, _ = proj.shape
    zero_init = state is None
    blk = lambda w, off: pl.BlockSpec((None, rows_in, w), lambda i, j: (i, j, off // w))
    const = lambda shape: pl.BlockSpec(shape, lambda i, j: (0,) * len(shape))
    in_specs = [blk(SSM_CONV_DIM, OFF_XBC), blk(SSM_INNER, OFF_SSMZ), blk(SMALL_W, OFF_SMALL),
                const((CONV_W, SSM_CONV_DIM)), const((1, SSM_CONV_DIM)), const((1, SMALL_W)), const((1, SMALL_W)),
                const((1, SMALL_W)), const((1, SSM_INNER))]
    args = [proj, proj, proj, conv_w, conv_b.reshape(1, SSM_CONV_DIM), _lane_vec(a_log, LANE_SSM_DT),
            _lane_vec(dt_bias, LANE_SSM_DT), _lane_vec(d_skip, 0), norm_g.reshape(1, SSM_INNER)]
    if not zero_init:
        in_specs += list(_state_specs(layer, conv_state, state))
        args += [conv_state, state]
    return pl.pallas_call(
        functools.partial(_ssd_kernel, rows_in=rows_in, rows=rows, zero_init=zero_init),
        grid=(b, l // rows_in),
        in_specs=in_specs,
        out_specs=[pl.BlockSpec((None, rows_in, SSM_INNER), lambda i, j: (i, j, 0)),
                   pl.BlockSpec((None, SSM_HEADS, SSM_P, SSM_N), lambda i, j: (i, 0, 0, 0))],
        out_shape=[jax.ShapeDtypeStruct((b, l, SSM_INNER), F32),
                   jax.ShapeDtypeStruct((b, SSM_HEADS, SSM_P, SSM_N), F32)],
        scratch_shapes=[pltpu.VMEM((8 + rows, SSM_CONV_DIM), F32), pltpu.VMEM((rows, SMALL_W), F32)],
        compiler_params=_cparams(("parallel", "arbitrary")),
        name="ssd",
    )(*args)


def _attend_blocks(q_ref, k_ref, v_ref, acc_ref, lse_ref, dil, blocks):
    span = HEAD
    scale = HEAD ** -0.5
    ii = _iota2((span, span), 0)
    jj = _iota2((span, span), 1)
    cur_ok = jj <= ii
    prev_ok = jj >= ii
    rows = lambda st: pl.ds(st, span) if dil == 1 else pl.ds(st, span, stride=dil)
    q = [q_ref[rows(st), :] for st, _, _ in blocks]
    s_c = [jnp.where(cur_ok, _dot_nt(q[i], k_ref[rows(st), :]) * scale, NEG_INF) for i, (st, _, _) in enumerate(blocks)]
    m = [jnp.max(s, axis=-1, keepdims=True) for s in s_c]
    s_p = []
    for i, (_, sp, ok) in enumerate(blocks):
        if sp is None:
            s_p.append(None)
            continue
        mask = prev_ok if ok is True else jnp.logical_and(prev_ok, ok)
        s_p.append(jnp.where(mask, _dot_nt(q[i], k_ref[rows(sp), :]) * scale, NEG_INF))
        m[i] = jnp.maximum(m[i], jnp.max(s_p[i], axis=-1, keepdims=True))
    for i, (st, sp, _) in enumerate(blocks):
        p_c = jnp.exp(s_c[i] - m[i])
        l = jnp.sum(p_c, axis=-1, keepdims=True)
        acc = _dot(p_c, v_ref[rows(st), :])
        if sp is not None:
            p_p = jnp.exp(s_p[i] - m[i])
            l = l + jnp.sum(p_p, axis=-1, keepdims=True)
            acc = acc + _dot(p_p, v_ref[rows(sp), :])
        acc_ref[rows(st), :] = acc / l
        lse_ref[rows(st), :] = jnp.broadcast_to(m[i] + jnp.log(l), (span, HEAD))


def _dil_prompt_kernel(*refs, seq):
    n_g = len(DIL_GROUPS)
    qs, ks, vs = refs[0:n_g], refs[n_g:2 * n_g], refs[2 * n_g:3 * n_g]
    o_ref, acc_s, lse_s = refs[3 * n_g:]
    u = ATTN_UNROLL
    for g, (_, dil) in enumerate(DIL_GROUPS):
        nb = seq // dil // HEAD
        attend = functools.partial(_attend_blocks, qs[g], ks[g], vs[g], acc_s.at[g], lse_s.at[g], dil)
        if dil == 1:
            assert nb % u == 0

            def body(it, carry, attend=attend):
                blocks = []
                for j in range(u):
                    st = pl.multiple_of((it * u + j) * HEAD, HEAD)
                    prev = pl.multiple_of(jnp.maximum(st - HEAD, 0), HEAD)
                    blocks.append((st, prev, (it > 0) if j == 0 else True))
                attend(blocks)
                return carry
            lax.fori_loop(0, nb // u, body, 0)
        elif nb > 1:
            def body(r, carry, attend=attend, dil=dil, nb=nb):
                attend([(r + n * dil * HEAD, None if n == 0 else r + (n - 1) * dil * HEAD, True) for n in range(nb)])
                return carry
            lax.fori_loop(0, dil, body, 0)
        else:
            assert dil % u == 0

            def body(it, carry, attend=attend):
                attend([(it * u + j, None, True) for j in range(u)])
                return carry
            lax.fori_loop(0, dil // u, body, 0)

    rows_per = 256

    def merge(it, carry):
        rows = pl.ds(pl.multiple_of(it * rows_per, rows_per), rows_per)
        lses = [lse_s[g, rows, :] for g in range(n_g)]
        top = functools.reduce(jnp.maximum, lses)
        wts = [jnp.exp(x - top) for x in lses]
        o_ref[rows, :] = sum(w * acc_s[g, rows, :] for g, w in enumerate(wts)) / sum(wts)
        return carry
    lax.fori_loop(0, seq // rows_per, merge, 0)


def dilated_prompt(proj):
    b, s, _ = proj.shape
    for _, dil in DIL_GROUPS:
        assert s % (dil * HEAD) == 0

    def src(off, g):
        base = off // HEAD + g * DIL_HPG
        return pl.BlockSpec((None, s, HEAD), lambda i, h: (i, 0, base + h))

    n_g = len(DIL_GROUPS)
    in_specs = [src(off, g) for off in (OFF_DILQ, OFF_DILK, OFF_DILV) for g in range(n_g)]
    return pl.pallas_call(
        functools.partial(_dil_prompt_kernel, seq=s),
        grid=(b, DIL_HPG),
        in_specs=in_specs,
        out_specs=pl.BlockSpec((None, s, HEAD), lambda i, h: (i, 0, h)),
        out_shape=jax.ShapeDtypeStruct((b, s, DIL_OUT), F32),
        scratch_shapes=[pltpu.VMEM((n_g, s, HEAD), F32), pltpu.VMEM((n_g, s, HEAD), F32)],
        compiler_params=_cparams(("parallel", "parallel")),
        name="dilated_prompt",
    )(*([proj] * (3 * n_g)))


def _dil_sample_kernel(new_ref, *rest, t_new):
    caches, o_ref = rest[:-1], rest[-1]
    nh = DIL_HPG
    nq = t_new * nh
    h_shift = int(math.log2(nh))
    n_g = len(DIL_GROUPS)
    scale = HEAD ** -0.5
    outs, lses = [], []
    for g, (_, dil) in enumerate(DIL_GROUPS):
        kc, vc = caches[2 * g], caches[2 * g + 1]
        q_all, k_new, v_new = new_ref[g], new_ref[n_g + g], new_ref[2 * n_g + g]
        if dil == 1:
            rounds = [(0, nq, nh, kc[...], vc[...])]
        else:
            rounds = [(8 * r, 8, 8, kc[:, 8 * r:8 * r + 8, :].reshape(kc.shape[0] * 8, HEAD),
                       vc[:, 8 * r:8 * r + 8, :].reshape(vc.shape[0] * 8, HEAD)) for r in range(nq // 8)]
        o_parts, lse_parts = [], []
        for row0, nr, width, k2d, v2d in rounds:
            q = q_all[row0:row0 + nr, :]
            n_keys = k2d.shape[0]
            qi = _iota2((nr, n_keys), 0) + row0
            kj = _iota2((nr, n_keys), 1)
            shift = int(math.log2(width))
            same_slot = (kj & (width - 1)) == (qi & (width - 1))
            first = (qi >> h_shift) >> int(math.log2(dil))
            ok = jnp.logical_and(same_slot, (kj >> shift) >= first)
            s_c = jnp.where(ok, _dot_nt(q, k2d) * scale, NEG_INF)
            qn = _iota2((nr, nq), 0) + row0
            kn = _iota2((nr, nq), 1)
            gap = (qn >> h_shift) - (kn >> h_shift)
            ok_n = jnp.logical_and((kn & (nh - 1)) == (qn & (nh - 1)),
                                   jnp.logical_and(gap >= 0, (gap & (dil - 1)) == 0))
            s_n = jnp.where(ok_n, _dot_nt(q, k_new) * scale, NEG_INF)
            m = jnp.maximum(jnp.max(s_c, axis=-1, keepdims=True), jnp.max(s_n, axis=-1, keepdims=True))
            p_c = jnp.exp(s_c - m)
            p_n = jnp.exp(s_n - m)
            l = jnp.sum(p_c, axis=-1, keepdims=True) + jnp.sum(p_n, axis=-1, keepdims=True)
            o_parts.append((_dot(p_c, v2d) + _dot(p_n, v_new)) / l)
            lse_parts.append(m + jnp.log(l))
        outs.append(o_parts[0] if len(o_parts) == 1 else jnp.concatenate(o_parts, axis=0))
        lses.append(lse_parts[0] if len(lse_parts) == 1 else jnp.concatenate(lse_parts, axis=0))
    top = functools.reduce(jnp.maximum, lses)
    wts = [jnp.exp(x - top) for x in lses]
    o_ref[...] = sum(w * o for w, o in zip(wts, outs)) / sum(wts)


def dilated_sample(proj, layer, caches):
    b, t_new, _ = proj.shape
    nh = DIL_HPG
    assert nh == 4 and (t_new * nh) % 8 == 0
    views, specs = [], []
    for g, (window, dil) in enumerate(DIL_GROUPS):
        for c in caches[2 * g:2 * g + 2]:
            lb = c.shape[2]
            assert lb == window and lb % dil == 0 and (dil == 1 or t_new <= dil) and dil & (dil - 1) == 0
            if dil == 1:
                views.append(c.reshape(c.shape[0], b, lb * nh, HEAD))
                specs.append(pl.BlockSpec((None, None, lb * nh, HEAD), lambda i: (layer, i, 0, 0)))
            else:
                views.append(c.reshape(c.shape[0], b, lb // dil, dil * nh, HEAD))
                specs.append(pl.BlockSpec((None, None, lb // dil, t_new * nh, HEAD), lambda i: (layer, i, 0, 0, 0)))
    n_g = len(DIL_GROUPS)
    new = proj[:, :, OFF_DILQ:OFF_DILQ + 3 * DIL_WIDTH].reshape(b, t_new, 3, n_g, nh, HEAD)
    new = new.transpose(0, 2, 3, 1, 4, 5).reshape(b, 3 * n_g, t_new * nh, HEAD)
    out = pl.pallas_call(
        functools.partial(_dil_sample_kernel, t_new=t_new),
        grid=(b,),
        in_specs=[pl.BlockSpec((None, 3 * n_g, t_new * nh, HEAD), lambda i: (i, 0, 0, 0))] + specs,
        out_specs=pl.BlockSpec((None, t_new * nh, HEAD), lambda i: (i, 0, 0)),
        out_shape=jax.ShapeDtypeStruct((b, t_new * nh, HEAD), F32),
        compiler_params=_cparams(("parallel",)),
        name="dilated_sample",
    )(new, *views)
    return out.reshape(b, t_new, DIL_OUT)


def _merge_kernel(x_ref, odn_ref, odil_ref, ossm_ref, g_dn, g_dil, g_ssm, w_dn, w_dil, w_ssm, w_out, gam_ref, o_ref):
    merged = (_sigmoid(g_dn[...]) * _dot(odn_ref[...], w_dn[...])
              + _sigmoid(g_dil[...]) * _dot(odil_ref[...], w_dil[...])
              + _sigmoid(g_ssm[...]) * _dot(ossm_ref[...], w_ssm[...]))
    y = _dot(merged, w_out[...])
    o_ref[...] = x_ref[...] + _rms_scale(y) * gam_ref[...]


def branch_merge(x, o_dn, o_dil, o_ssm, proj, w_dn, w_dil, w_ssm, w_out, gam, *, tm):
    n = x.shape[0]
    row = lambda w: pl.BlockSpec((tm, w), lambda i: (i, 0))
    gate = lambda k: pl.BlockSpec((tm, D_MODEL), lambda i: (i, OFF_GATES // D_MODEL + k))
    const = lambda a: pl.BlockSpec(a.shape, lambda i: (0,) * a.ndim)
    gam = gam.reshape(1, D_MODEL)
    return pl.pallas_call(
        _merge_kernel,
        grid=(n // tm,),
        in_specs=[row(D_MODEL), row(DN_WIDTH), row(DIL_OUT), row(SSM_INNER), gate(0), gate(1), gate(2),
                  const(w_dn), const(w_dil), const(w_ssm), const(w_out), const(gam)],
        out_specs=row(D_MODEL),
        out_shape=jax.ShapeDtypeStruct((n, D_MODEL), F32),
        compiler_params=_cparams(("parallel",)),
        name="branch_merge",
    )(x, o_dn, o_dil, o_ssm, proj, proj, proj, w_dn, w_dil, w_ssm, w_out, gam)


def _mem_attn_kernel(q_ref, k_ref, v_ref, o_ref, qbuf, *, rows_in, rows, head_major):
    scale = HEAD ** -0.5
    if rows_in < rows:
        qbuf[...] = jnp.zeros(qbuf.shape, F32)
        qbuf[0:rows_in, :] = q_ref[...]
        q_all = qbuf
    else:
        q_all = q_ref
    for h in range(MEM_HEADS):
        cols = slice(h * HEAD, (h + 1) * HEAD)
        k = k_ref[:, h, :] if head_major else k_ref[:, cols]
        v = v_ref[:, h, :] if head_major else v_ref[:, cols]
        s = _dot_nt(q_all[:, cols], k) * scale
        m = jnp.max(s, axis=-1, keepdims=True)
        p = jnp.exp(s - m)
        o = _dot(p, v) / jnp.sum(p, axis=-1, keepdims=True)
        o_ref[:, cols] = o[0:rows_in, :]


def mem_attention(q, k, v, kv_specs, *, tq, head_major):
    b, l, _ = q.shape
    rows = max(tq, 8)
    return pl.pallas_call(
        functools.partial(_mem_attn_kernel, rows_in=tq, rows=rows, head_major=head_major),
        grid=(b, l // tq),
        in_specs=[pl.BlockSpec((None, tq, MEM_WIDTH), lambda i, j: (i, j, 0))] + list(kv_specs),
        out_specs=pl.BlockSpec((None, tq, MEM_WIDTH), lambda i, j: (i, j, 0)),
        out_shape=jax.ShapeDtypeStruct((b, l, MEM_WIDTH), F32),
        scratch_shapes=[pltpu.VMEM((rows, MEM_WIDTH), F32)],
        compiler_params=_cparams(("parallel", "parallel")),
        name="mem_attention",
    )(q, k, v)


def _matmul_norm_res_kernel(x_ref, a_ref, w_ref, gam_ref, o_ref):
    y = _dot(a_ref[...], w_ref[...])
    o_ref[...] = x_ref[...] + _rms_scale(y) * gam_ref[...]


def matmul_norm_residual(x, a, w, gam, *, tm):
    n, k = a.shape
    return pl.pallas_call(
        _matmul_norm_res_kernel,
        grid=(n // tm,),
        in_specs=[pl.BlockSpec((tm, D_MODEL), lambda i: (i, 0)), pl.BlockSpec((tm, k), lambda i: (i, 0)),
                  pl.BlockSpec((k, D_MODEL), lambda i: (0, 0)), pl.BlockSpec((1, D_MODEL), lambda i: (0, 0))],
        out_specs=pl.BlockSpec((tm, D_MODEL), lambda i: (i, 0)),
        out_shape=jax.ShapeDtypeStruct((n, D_MODEL), F32),
        compiler_params=_cparams(("parallel",)),
        name="matmul_norm_residual",
    )(x, a, w, gam.reshape(1, D_MODEL))


def _mlp_kernel(x_ref, gpre_ref, w1_ref, w2_ref, gpost_ref, o_ref, h_ref, acc_ref):
    k = pl.program_id(1)

    @pl.when(k == 0)
    def _():
        h_ref[...] = (_rms_scale(x_ref[...]) * gpre_ref[...]).astype(BF16)
        acc_ref[...] = jnp.zeros(acc_ref.shape, F32)

    f = jnp.maximum(jnp.dot(h_ref[...], w1_ref[...], preferred_element_type=F32), 0.0)
    acc_ref[...] += _dot(f * f, w2_ref[...])

    @pl.when(k == pl.num_programs(1) - 1)
    def _():
        o_ref[...] = x_ref[...] + _rms_scale(acc_ref[...]) * gpost_ref[...]


def mlp(x, gpre, w1, w2, gpost, *, tm, tk):
    n = x.shape[0]
    return pl.pallas_call(
        _mlp_kernel,
        grid=(n // tm, D_FF // tk),
        in_specs=[pl.BlockSpec((tm, D_MODEL), lambda i, k: (i, 0)), pl.BlockSpec((1, D_MODEL), lambda i, k: (0, 0)),
                  pl.BlockSpec((D_MODEL, tk), lambda i, k: (0, k)), pl.BlockSpec((tk, D_MODEL), lambda i, k: (k, 0)),
                  pl.BlockSpec((1, D_MODEL), lambda i, k: (0, 0))],
        out_specs=pl.BlockSpec((tm, D_MODEL), lambda i, k: (i, 0)),
        out_shape=jax.ShapeDtypeStruct((n, D_MODEL), F32),
        scratch_shapes=[pltpu.VMEM((tm, D_MODEL), BF16), pltpu.VMEM((tm, D_MODEL), F32)],
        compiler_params=_cparams(("parallel", "arbitrary")),
        name="mlp",
    )(x, gpre.reshape(1, D_MODEL), w1, w2, gpost.reshape(1, D_MODEL))


def reorder_w_in(w_in):
    idx = []
    acc = 0
    for s in IN_SIZES:
        idx.append((acc, acc + s))
        acc += s
    dn_qkv, dn_z, dn_b, dn_a, dil_qkv, ssm_z, ssm_xbc, ssm_dt, gates = [w_in[:, a:b] for a, b in idx]
    small = jnp.concatenate([dn_b, dn_a, ssm_dt], axis=1)
    pad = jnp.zeros((w_in.shape[0], NP - OFF_SMALL - small.shape[1]), w_in.dtype)
    return jnp.concatenate([dn_qkv, dn_z, ssm_xbc, ssm_z, gates, dil_qkv, small, pad], axis=1).astype(BF16)


def trunk_layer(x, layer, rope, mem, states, p, cfg):
    b, l, _ = x.shape
    n = b * l
    tm = cfg["tm"]
    xf = x.reshape(n, D_MODEL)
    proj = norm_matmul(xf, p["norm_mix_pre"], p["w_in"], tm=cfg["tm_in"], tn=1024,
                       rope=rope + (OFF_DILQ // 1024, OFF_DILV // 1024))
    proj3 = proj.reshape(b, l, NP)
    dn_conv, dn_state, ssm_conv, ssm_state = states[:4] if states is not None else (None,) * 4

    o_dn, dn_state_new = deltanet(proj3, layer, dn_conv, dn_state, p["dn_conv_w"], p["dn_a_log"], p["dn_dt_bias"],
                                  p["dn_norm"], rows_in=cfg["rows_in"], rows=cfg["rows"])
    o_ssm, ssm_state_new = ssd(proj3, layer, ssm_conv, ssm_state, p["ssm_conv_w"], p["ssm_conv_b"], p["ssm_a_log"],
                               p["ssm_dt_bias"], p["ssm_d"], p["ssm_norm"], rows_in=cfg["rows_in_ssd"],
                               rows=cfg["rows_ssd"])
    win_new = []
    o_dil = dilated_prompt(proj3) if states is None else dilated_sample(proj3, layer, states[4:])
    for gi, (window, _) in enumerate(DIL_GROUPS):
        keep = min(window, l) if states is None else l
        for off in (OFF_DILK, OFF_DILV):
            c0 = off + gi * DIL_OUT
            win_new.append(proj3[:, l - keep:, c0:c0 + DIL_OUT].reshape(b, keep, DIL_HPG, HEAD))

    xf = branch_merge(xf, o_dn.reshape(n, DN_WIDTH), o_dil.reshape(n, DIL_OUT), o_ssm.reshape(n, SSM_INNER), proj,
                      p["w_br_dn"], p["w_br_dil"], p["w_br_ssm"], p["w_out"], p["norm_mix_post"], tm=cfg["tm_merge"])

    qm = norm_matmul(xf, p["norm_mem_pre"], p["w_mq"], tm=tm, tn=MEM_WIDTH)
    mem_k, mem_v, mem_specs, head_major = mem
    om = mem_attention(qm.reshape(b, l, MEM_WIDTH), mem_k, mem_v, mem_specs, tq=cfg["tq"], head_major=head_major)
    xf = matmul_norm_residual(xf, om.reshape(n, MEM_WIDTH), p["w_mo"], p["norm_mem_post"], tm=tm)

    xf = mlp(xf, p["norm_ffn_pre"], p["w_ff1"], p["w_ff2"], p["norm_ffn_post"], tm=tm, tk=1024)

    def conv_tail(buf, off, width):
        new = proj3[:, :, off:off + width]
        if l < CONV_W - 1:
            prev = jnp.zeros((b, CONV_W - 1, width), F32) if buf is None else buf[layer]
            new = jnp.concatenate([prev, new], axis=1)
        return new[:, new.shape[1] - (CONV_W - 1):]

    new_states = (conv_tail(dn_conv, OFF_DNQKV, 3 * DN_WIDTH), dn_state_new,
                  conv_tail(ssm_conv, OFF_XBC, SSM_CONV_DIM), ssm_state_new) + tuple(win_new)
    return xf.reshape(b, l, D_MODEL), new_states


def kernel(x_prompt, x_sample, state_dn_conv, state_dn, state_ssm_conv, state_ssm, cache_win1_k, cache_win1_v, cache_win2_k, cache_win2_v, cache_win3_k, cache_win3_v, cache_mem_k, cache_mem_v, mem_prompt, norm_mix_pre, w_in, dn_conv_w, dn_a_log, dn_dt_bias, dn_norm, ssm_conv_w, ssm_conv_b, ssm_a_log, ssm_dt_bias, ssm_d, ssm_norm, w_br_dn, w_br_dil, w_br_ssm, w_out, norm_mix_post, norm_mem_pre, norm_mem_kv, w_mq, w_mkv, w_mo, norm_mem_post, norm_ffn_pre, w_ff1, w_ff2, norm_ffn_post):
    n_p, s = x_prompt.shape[:2]
    n_s, t = x_sample.shape[:2]
    n_mem = mem_prompt.shape[1]
    depth = w_in.shape[0]
    tm_p, tm_s = 1024, n_s * t
    rope_p = rope_tables(jnp.arange(s, dtype=F32))
    rope_s = rope_tables(jnp.tile(PAST_LEN + jnp.arange(t, dtype=F32), tm_s // t))
    cfg_p = dict(tm=tm_p, tm_in=2 * tm_p, tm_merge=256, rows_in=64, rows=64, rows_in_ssd=128, rows_ssd=128, tq=512)
    cfg_s = dict(tm=tm_s, tm_in=tm_s, tm_merge=tm_s, rows_in=t, rows=8, rows_in_ssd=t, rows_ssd=8, tq=t)
    states_s = (state_dn_conv, state_dn, state_ssm_conv, state_ssm, cache_win1_k, cache_win1_v, cache_win2_k,
                cache_win2_v, cache_win3_k, cache_win3_v)
    xp, xs = x_prompt, x_sample
    new_p = [[] for _ in range(12)]
    new_s = [[] for _ in range(10)]
    for l in range(depth):
        bf = lambda a: a[l].astype(BF16)
        prm = dict(norm_mix_pre=norm_mix_pre[l], w_in=reorder_w_in(w_in[l]), dn_conv_w=dn_conv_w[l],
                   dn_a_log=dn_a_log[l], dn_dt_bias=dn_dt_bias[l], dn_norm=dn_norm[l], ssm_conv_w=ssm_conv_w[l],
                   ssm_conv_b=ssm_conv_b[l], ssm_a_log=ssm_a_log[l], ssm_dt_bias=ssm_dt_bias[l], ssm_d=ssm_d[l],
                   ssm_norm=ssm_norm[l], w_br_dn=bf(w_br_dn), w_br_dil=bf(w_br_dil), w_br_ssm=bf(w_br_ssm),
                   w_out=bf(w_out), norm_mix_post=norm_mix_post[l], norm_mem_pre=norm_mem_pre[l], w_mq=bf(w_mq),
                   w_mo=bf(w_mo), norm_mem_post=norm_mem_post[l], norm_ffn_pre=norm_ffn_pre[l], w_ff1=bf(w_ff1),
                   w_ff2=bf(w_ff2), norm_ffn_post=norm_ffn_post[l])
        mkv = norm_matmul(mem_prompt.reshape(n_p * n_mem, D_MODEL), norm_mem_kv[l], bf(w_mkv),
                          tm=min(1024, n_p * n_mem), tn=1024)
        mkv = mkv.reshape(n_p, n_mem, 2 * MEM_WIDTH)
        mk = mkv[:, :, :MEM_WIDTH].reshape(n_p, n_mem, MEM_HEADS, HEAD)
        mv = mkv[:, :, MEM_WIDTH:].reshape(n_p, n_mem, MEM_HEADS, HEAD)
        mem_p = (mkv, mkv, [pl.BlockSpec((None, n_mem, MEM_WIDTH), lambda i, j: (i, 0, 0)),
                            pl.BlockSpec((None, n_mem, MEM_WIDTH), lambda i, j: (i, 0, 1))], False)
        xp, st_p = trunk_layer(xp, l, rope_p, mem_p, None, prm, cfg_p)
        for i, a in enumerate(st_p + (mk, mv)):
            new_p[i].append(a)
        cache_spec = pl.BlockSpec((None, None, n_mem, MEM_HEADS, HEAD), lambda i, j, l=l: (l, i, 0, 0, 0))
        mem_s = (cache_mem_k, cache_mem_v, [cache_spec, cache_spec], True)
        xs, st_s = trunk_layer(xs, l, rope_s, mem_s, states_s, prm, cfg_s)
        for i, a in enumerate(st_s):
            new_s[i].append(a)
    outs_p = [jnp.stack(a) for a in new_p]
    outs_s = [jnp.stack(a) for a in new_s]
    return (xp, xs, *outs_p, *outs_s)
```

```python
import functools
import math

import jax
import jax.numpy as jnp
from jax import lax
from jax.experimental import pallas as pl
from jax.experimental.pallas import tpu as pltpu

F32 = jnp.float32
BF16 = jnp.bfloat16

D_MODEL = 1024
CONV_W = 4
DN_HEADS = 8
DN_DK = 128
DN_WIDTH = DN_HEADS * DN_DK
DIL_GROUPS = ((128, 1), (512, 4), (2048, 16))
DIL_HPG = 4
HEAD = 128
DIL_HEADS = DIL_HPG * len(DIL_GROUPS)
DIL_WIDTH = DIL_HEADS * HEAD
DIL_OUT = DIL_HPG * HEAD
ROT_DIM = HEAD // 4
ROPE_THETA = 500000.0
PAST_LEN = 2048
SSM_HEADS = 16
SSM_P = 64
SSM_N = 128
SSM_GROUPS = 4
SSM_HPG = SSM_HEADS // SSM_GROUPS
SSM_INNER = SSM_HEADS * SSM_P
SSM_CONV_DIM = SSM_INNER + 2 * SSM_GROUPS * SSM_N
MEM_HEADS = 4
MEM_WIDTH = MEM_HEADS * HEAD
D_FF = 4 * D_MODEL
EPS = 1e-6
NEG_INF = -1e30
IN_SIZES = (3 * DN_WIDTH, DN_WIDTH, DN_HEADS, DN_HEADS, 3 * DIL_WIDTH, SSM_INNER, SSM_CONV_DIM, SSM_HEADS,
            3 * D_MODEL)

OFF_DNQKV = 0
OFF_DNZ = OFF_DNQKV + 3 * DN_WIDTH
OFF_XBC = OFF_DNZ + DN_WIDTH
OFF_SSMZ = OFF_XBC + SSM_CONV_DIM
OFF_GATES = OFF_SSMZ + SSM_INNER
OFF_DILQ = OFF_GATES + 3 * D_MODEL
OFF_DILK = OFF_DILQ + DIL_WIDTH
OFF_DILV = OFF_DILK + DIL_WIDTH
OFF_SMALL = OFF_DILV + DIL_WIDTH
SMALL_W = 128
NP = 15 * 1024
OFF_B = OFF_DILQ
LANE_DN_B = 0
LANE_DN_A = DN_HEADS
LANE_SSM_DT = 2 * DN_HEADS

SOLVE_BASE = 16
ATTN_UNROLL = 4

VMEM_LIMIT = 56 * 1024 * 1024


def _cparams(sem):
    return pltpu.CompilerParams(dimension_semantics=sem, vmem_limit_bytes=VMEM_LIMIT)


def _dot(a, b):
    return jnp.dot(a.astype(BF16), b.astype(BF16), preferred_element_type=F32)


def _dot_nt(a, b):
    return lax.dot_general(a.astype(BF16), b.astype(BF16), (((1,), (1,)), ((), ())), preferred_element_type=F32)


def _dot_tn(a, b):
    return lax.dot_general(a.astype(BF16), b.astype(BF16), (((0,), (0,)), ((), ())), preferred_element_type=F32)


def _split3(x):
    hi = x.astype(BF16)
    r = x - hi.astype(F32)
    mid = r.astype(BF16)
    lo = (r - mid.astype(F32)).astype(BF16)
    return hi, mid, lo


def _sel_dot(sel, x):
    return sum(jnp.dot(sel, p, preferred_element_type=F32) for p in _split3(x))


def _sel_dot_nt(sel, x):
    return sum(lax.dot_general(sel, p, (((1,), (1,)), ((), ())), preferred_element_type=F32) for p in _split3(x))


def _rms_scale(x):
    return x * lax.rsqrt(jnp.mean(x * x, axis=-1, keepdims=True) + EPS)


def _softplus(x):
    return jnp.maximum(x, 0.0) + jnp.log1p(jnp.exp(-jnp.abs(x)))


def _sigmoid(x):
    return 1.0 / (1.0 + jnp.exp(-x))


def _silu(x):
    return x * _sigmoid(x)


def _iota2(shape, axis):
    return lax.broadcasted_iota(jnp.int32, shape, axis)


def _norm_matmul_kernel(*refs, rope_tiles, tn):
    if rope_tiles is None:
        x_ref, g_ref, w_ref, o_ref, h_ref = refs
    else:
        x_ref, g_ref, w_ref, cos_ref, sa_ref, sb_ref, o_ref, h_ref = refs
    j = pl.program_id(1)

    @pl.when(j == 0)
    def _():
        h_ref[...] = (_rms_scale(x_ref[...]) * g_ref[...]).astype(BF16)

    y = jnp.dot(h_ref[...], w_ref[...], preferred_element_type=F32)
    if rope_tiles is None:
        o_ref[...] = y.astype(o_ref.dtype)
        return
    lo, hi = rope_tiles
    is_rope = jnp.logical_and(j >= lo, j < hi)

    @pl.when(is_rope)
    def _():
        cos, sa, sb = cos_ref[...], sa_ref[...], sb_ref[...]
        for c in range(tn // HEAD):
            t = y[:, c * HEAD:(c + 1) * HEAD]
            o_ref[:, c * HEAD:(c + 1) * HEAD] = (
                t * cos + pltpu.roll(t, HEAD - ROT_DIM // 2, 1) * sa + pltpu.roll(t, ROT_DIM // 2, 1) * sb)

    @pl.when(jnp.logical_not(is_rope))
    def _():
        o_ref[...] = y


def norm_matmul(x, g, w, *, tm, tn, rope=None, out_dtype=F32):
    n, k = x.shape
    m = w.shape[1]
    assert n % tm == 0 and m % tn == 0
    in_specs = [pl.BlockSpec((tm, k), lambda i, j: (i, 0)),
                pl.BlockSpec((1, k), lambda i, j: (0, 0)),
                pl.BlockSpec((k, tn), lambda i, j: (0, j))]
    args = [x, g.reshape(1, k), w]
    rope_tiles = None
    if rope is not None:
        cos, sa, sb, lo, hi = rope
        period = cos.shape[0] // tm
        assert cos.shape[0] % tm == 0
        tab = pl.BlockSpec((tm, HEAD), lambda i, j: (i % period, 0))
        in_specs += [tab, tab, tab]
        args += [cos, sa, sb]
        rope_tiles = (lo, hi)
    return pl.pallas_call(
        functools.partial(_norm_matmul_kernel, rope_tiles=rope_tiles, tn=tn),
        grid=(n // tm, m // tn),
        in_specs=in_specs,
        out_specs=pl.BlockSpec((tm, tn), lambda i, j: (i, j)),
        out_shape=jax.ShapeDtypeStruct((n, m), out_dtype),
        scratch_shapes=[pltpu.VMEM((tm, k), BF16)],
        compiler_params=_cparams(("parallel", "arbitrary")),
        name="norm_matmul",
    )(*args)


def rope_tables(pos):
    half = ROT_DIM // 2
    inv = jnp.power(ROPE_THETA, -jnp.arange(half, dtype=F32) * 2.0 / ROT_DIM)
    ang = pos[:, None] * inv[None, :]
    cos, sin = jnp.cos(ang), jnp.sin(ang)
    n = pos.shape[0]
    ones = jnp.ones((n, HEAD - ROT_DIM), F32)
    zeros = jnp.zeros((n, HEAD - ROT_DIM), F32)
    z16 = jnp.zeros((n, half), F32)
    return (jnp.concatenate([cos, cos, ones], axis=1),
            jnp.concatenate([-sin, z16, zeros], axis=1),
            jnp.concatenate([z16, sin, zeros], axis=1))


def _conv_cols(xbuf, cw_ref, c0, width, rows, bias_ref=None):
    acc = xbuf[5:5 + rows, c0:c0 + width] * cw_ref[0:1, c0:c0 + width]
    for j in range(1, CONV_W):
        acc = acc + xbuf[5 + j:5 + j + rows, c0:c0 + width] * cw_ref[j:j + 1, c0:c0 + width]
    if bias_ref is not None:
        acc = acc + bias_ref[0:1, c0:c0 + width]
    return _silu(acc)


def _stage_rows(xbuf, smbuf, parts, sm_ref, cs_ref, rows_in, rows):
    @pl.when(pl.program_id(1) == 0)
    def _():
        xbuf[5:8, :] = jnp.zeros((CONV_W - 1, xbuf.shape[1]), F32) if cs_ref is None else cs_ref[...]

    @pl.when(pl.program_id(1) > 0)
    def _():
        xbuf[5:8, :] = xbuf[5 + rows_in:8 + rows_in, :]

    for ref, c0 in parts:
        xbuf[8:8 + rows_in, c0:c0 + ref.shape[1]] = ref[...].astype(F32)
    if rows_in < rows:
        xbuf[8 + rows_in:8 + rows, :] = jnp.zeros((rows - rows_in, xbuf.shape[1]), F32)
        smbuf[...] = jnp.zeros(smbuf.shape, F32)
        smbuf[0:rows_in, :] = sm_ref[...]
        return smbuf[...]
    return sm_ref[...]


def _lane_vec(v, lane0):
    return jnp.zeros((1, SMALL_W), F32).at[0, lane0:lane0 + v.shape[0]].set(v)


def _state_specs(layer, conv_state, state):
    cs = pl.BlockSpec((None, None) + conv_state.shape[2:], lambda i, j: (layer, i, 0, 0))
    st = pl.BlockSpec((None, None) + state.shape[2:], lambda i, j: (layer, i, 0, 0, 0))
    return cs, st


def _dn_kernel(*refs, rows_in, rows, zero_init):
    if zero_init:
        q_ref, k_ref, v_ref, z_ref, sm_ref, cw_ref, alog_ref, dtb_ref, gam_ref, o_ref, st_ref, xbuf, smbuf = refs
        cs_ref = s0_ref = None
    else:
        (q_ref, k_ref, v_ref, z_ref, sm_ref, cw_ref, alog_ref, dtb_ref, gam_ref, cs_ref, s0_ref,
         o_ref, st_ref, xbuf, smbuf) = refs
    c = rows
    heads = range(DN_HEADS)

    @pl.when(pl.program_id(1) == 0)
    def _():
        st_ref[...] = jnp.zeros(st_ref.shape, F32) if zero_init else s0_ref[...]

    sm = _stage_rows(xbuf, smbuf, ((q_ref, 0), (k_ref, DN_WIDTH), (v_ref, 2 * DN_WIDTH)), sm_ref, cs_ref,
                     rows_in, rows)
    beta_all = _sigmoid(sm)
    g_all = -jnp.exp(alog_ref[...]) * _softplus(sm + dtb_ref[...])
    if rows_in < rows:
        live = (_iota2((c, 1), 0) < rows_in).astype(F32)
        beta_all = beta_all * live
        g_all = g_all * live
    ii = _iota2((c, c), 0)
    jj = _iota2((c, c), 1)
    tri = ii >= jj
    strict = ii > jj
    eye = (ii == jj).astype(F32)
    gcum_all = _sel_dot(tri.astype(BF16), g_all)
    sel = (_iota2((DN_HEADS, SMALL_W), 1) == _iota2((DN_HEADS, SMALL_W), 0) + LANE_DN_A).astype(BF16)
    grow_all = _sel_dot_nt(sel, gcum_all)
    base = min(c, SOLVE_BASE)
    n_sq = int(math.log2(base)) - 1
    n_lvl = int(math.log2(c // base))
    assert 2 ** (n_sq + 1) == base and base * 2 ** n_lvl == c
    same_blk = [(ii >> (n_sq + 1 + lvl)) == (jj >> (n_sq + 1 + lvl)) for lvl in range(n_lvl + 1)]

    q = [_conv_cols(xbuf, cw_ref, h * DN_DK, DN_DK, c) for h in heads]
    k = [_conv_cols(xbuf, cw_ref, DN_WIDTH + h * DN_DK, DN_DK, c) for h in heads]
    v = [_conv_cols(xbuf, cw_ref, 2 * DN_WIDTH + h * DN_DK, DN_DK, c) for h in heads]
    q = [x * lax.rsqrt(jnp.sum(x * x, axis=-1, keepdims=True) + EPS) * (DN_DK ** -0.5) for x in q]
    k = [x * lax.rsqrt(jnp.sum(x * x, axis=-1, keepdims=True) + EPS) for x in k]
    gc = [gcum_all[:, LANE_DN_A + h:LANE_DN_A + h + 1] for h in heads]
    gr = [grow_all[h:h + 1, :] for h in heads]
    beta = [beta_all[:, LANE_DN_B + h:LANE_DN_B + h + 1] for h in heads]
    decay = [jnp.where(tri, jnp.exp(jnp.where(tri, gc[h] - gr[h], 0.0)), 0.0) for h in heads]
    eg = [jnp.exp(gc[h]) for h in heads]
    kb = [k[h] * beta[h] for h in heads]
    a_mat = [_dot_nt(kb[h], k[h]) * jnp.where(strict, decay[h], 0.0) for h in heads]
    qk = [_dot_nt(q[h], k[h]) * decay[h] for h in heads]
    x = [-jnp.where(same_blk[0], a_mat[h], 0.0) for h in heads]
    t_inv = [eye + x[h] for h in heads]
    for _ in range(n_sq):
        x = [_dot(x[h], x[h]) for h in heads]
        t_inv = [t_inv[h] + _dot(t_inv[h], x[h]) for h in heads]
    for lvl in range(1, len(same_blk)):
        ring = jnp.logical_and(same_blk[lvl], jnp.logical_not(same_blk[lvl - 1]))
        left = [_dot(t_inv[h], jnp.where(ring, a_mat[h], 0.0)) for h in heads]
        t_inv = [t_inv[h] - _dot(left[h], t_inv[h]) for h in heads]
    sol = [_dot(t_inv[h], jnp.concatenate([v[h] * beta[h], kb[h] * eg[h]], axis=1)) for h in heads]
    s_prev = [st_ref[h] for h in heads]
    v_new = [sol[h][:, :DN_DK] - _dot(sol[h][:, DN_DK:], s_prev[h]) for h in heads]
    o = [_dot(q[h] * eg[h], s_prev[h]) + _dot(qk[h], v_new[h]) for h in heads]
    g_last = [gc[h][c - 1:c, :] for h in heads]
    k_dec = [k[h] * jnp.exp(g_last[h] - gc[h]) for h in heads]
    s_new = [s_prev[h] * jnp.exp(g_last[h]) + _dot_tn(k_dec[h], v_new[h]) for h in heads]
    for h in heads:
        st_ref[h] = s_new[h]
        z = z_ref[:, h * DN_DK:(h + 1) * DN_DK].astype(F32)
        o_ref[:, h * DN_DK:(h + 1) * DN_DK] = _rms_scale(o[h][0:rows_in, :]) * gam_ref[...] * _silu(z)


def deltanet(proj, proj_b, layer, conv_state, state, conv_w, a_log, dt_bias, norm_g, *, rows_in, rows):
    b, l, _ = proj.shape
    assert l % rows_in == 0 and rows_in <= rows
    zero_init = state is None
    blk = lambda w, off: pl.BlockSpec((None, rows_in, w), lambda i, j: (i, j, off // w))
    const = lambda shape: pl.BlockSpec(shape, lambda i, j: (0,) * len(shape))
    in_specs = [blk(DN_WIDTH, OFF_DNQKV), blk(DN_WIDTH, OFF_DNQKV + DN_WIDTH), blk(DN_WIDTH, OFF_DNQKV + 2 * DN_WIDTH),
                blk(DN_WIDTH, OFF_DNZ), blk(SMALL_W, OFF_SMALL - OFF_B),
                const((CONV_W, 3 * DN_WIDTH)), const((1, SMALL_W)), const((1, SMALL_W)), const((1, DN_DK))]
    args = [proj, proj, proj, proj, proj_b, conv_w, _lane_vec(a_log, LANE_DN_A), _lane_vec(dt_bias, LANE_DN_A),
            norm_g.reshape(1, DN_DK)]
    if not zero_init:
        in_specs += list(_state_specs(layer, conv_state, state))
        args += [conv_state, state]
    return pl.pallas_call(
        functools.partial(_dn_kernel, rows_in=rows_in, rows=rows, zero_init=zero_init),
        grid=(b, l // rows_in),
        in_specs=in_specs,
        out_specs=[pl.BlockSpec((None, rows_in, DN_WIDTH), lambda i, j: (i, j, 0)),
                   pl.BlockSpec((None, DN_HEADS, DN_DK, DN_DK), lambda i, j: (i, 0, 0, 0))],
        out_shape=[jax.ShapeDtypeStruct((b, l, DN_WIDTH), F32),
                   jax.ShapeDtypeStruct((b, DN_HEADS, DN_DK, DN_DK), F32)],
        scratch_shapes=[pltpu.VMEM((8 + rows, 3 * DN_WIDTH), F32), pltpu.VMEM((rows, SMALL_W), F32)],
        compiler_params=_cparams(("parallel", "arbitrary")),
        name="deltanet",
    )(*args)


def _ssd_kernel(*refs, rows_in, rows, zero_init):
    if zero_init:
        x_ref, z_ref, sm_ref, cw_ref, cb_ref, alog_ref, dtb_ref, d_ref, gam_ref, o_ref, st_ref, xbuf, smbuf = refs
        cs_ref = h0_ref = None
    else:
        (x_ref, z_ref, sm_ref, cw_ref, cb_ref, alog_ref, dtb_ref, d_ref, gam_ref, cs_ref, h0_ref,
         o_ref, st_ref, xbuf, smbuf) = refs
    c = rows
    groups = range(SSM_GROUPS)
    heads = range(SSM_HEADS)

    @pl.when(pl.program_id(1) == 0)
    def _():
        st_ref[...] = jnp.zeros(st_ref.shape, F32) if zero_init else h0_ref[...]

    sm = _stage_rows(xbuf, smbuf, ((x_ref, 0),), sm_ref, cs_ref, rows_in, rows)
    dt_all = _softplus(sm + dtb_ref[...])
    if rows_in < rows:
        dt_all = dt_all * (_iota2((c, 1), 0) < rows_in).astype(F32)
    da_all = dt_all * (-jnp.exp(alog_ref[...]))
    ii = _iota2((c, c), 0)
    jj = _iota2((c, c), 1)
    tri = ii >= jj
    acum_all = _sel_dot(tri.astype(BF16), da_all)
    sel = (_iota2((SSM_HEADS, SMALL_W), 1) == _iota2((SSM_HEADS, SMALL_W), 0) + LANE_SSM_DT).astype(BF16)
    arow_all = _sel_dot_nt(sel, acum_all)
    dtrow_all = _sel_dot_nt(sel, dt_all)
    gn = SSM_GROUPS * SSM_N
    gw = SSM_HPG * SSM_P

    bm = [_conv_cols(xbuf, cw_ref, SSM_INNER + g * SSM_N, SSM_N, c, cb_ref) for g in groups]
    cm = [_conv_cols(xbuf, cw_ref, SSM_INNER + gn + g * SSM_N, SSM_N, c, cb_ref) for g in groups]
    xg = [_conv_cols(xbuf, cw_ref, g * gw, gw, c, cb_ref) for g in groups]
    cb = [_dot_nt(cm[g], bm[g]) for g in groups]
    grp = [h // SSM_HPG for h in heads]
    xh = [xg[grp[h]][:, (h % SSM_HPG) * SSM_P:(h % SSM_HPG + 1) * SSM_P] for h in heads]
    ac = [acum_all[:, LANE_SSM_DT + h:LANE_SSM_DT + h + 1] for h in heads]
    dc = [dt_all[:, LANE_SSM_DT + h:LANE_SSM_DT + h + 1] for h in heads]
    lmat = [jnp.where(tri, jnp.exp(jnp.where(tri, ac[h] - arow_all[h:h + 1, :], 0.0)), 0.0) for h in heads]
    h_prev = [st_ref[h] for h in heads]
    y = [_dot(cb[grp[h]] * lmat[h] * dtrow_all[h:h + 1, :], xh[h])
         + _dot_nt(cm[grp[h]], h_prev[h]) * jnp.exp(ac[h]) + d_ref[0:1, h:h + 1] * xh[h] for h in heads]
    a_last = [ac[h][c - 1:c, :] for h in heads]
    wdec = [dc[h] * jnp.exp(a_last[h] - ac[h]) for h in heads]
    h_new = [h_prev[h] * jnp.exp(a_last[h]) + _dot_tn(xh[h] * wdec[h], bm[grp[h]]) for h in heads]
    for h in heads:
        st_ref[h] = h_new[h]
    for g in groups:
        yg = jnp.concatenate(y[g * SSM_HPG:(g + 1) * SSM_HPG], axis=1)
        z = z_ref[:, g * gw:(g + 1) * gw].astype(F32)
        o_ref[:, g * gw:(g + 1) * gw] = _rms_scale(yg[0:rows_in, :] * _silu(z)) * gam_ref[0:1, g * gw:(g + 1) * gw]


def ssd(proj, proj_b, layer, conv_state, state, conv_w, conv_b, a_log, dt_bias, d_skip, norm_g, *, rows_in, rows):
    b, l, _ = proj.shape
    zero_init = state is None
    blk = lambda w, off: pl.BlockSpec((None, rows_in, w), lambda i, j: (i, j, off // w))
    const = lambda shape: pl.BlockSpec(shape, lambda i, j: (0,) * len(shape))
    in_specs = [blk(SSM_CONV_DIM, OFF_XBC), blk(SSM_INNER, OFF_SSMZ), blk(SMALL_W, OFF_SMALL - OFF_B),
                const((CONV_W, SSM_CONV_DIM)), const((1, SSM_CONV_DIM)), const((1, SMALL_W)), const((1, SMALL_W)),
                const((1, SMALL_W)), const((1, SSM_INNER))]
    args = [proj, proj, proj_b, conv_w, conv_b.reshape(1, SSM_CONV_DIM), _lane_vec(a_log, LANE_SSM_DT),
            _lane_vec(dt_bias, LANE_SSM_DT), _lane_vec(d_skip, 0), norm_g.reshape(1, SSM_INNER)]
    if not zero_init:
        in_specs += list(_state_specs(layer, conv_state, state))
        args += [conv_state, state]
    return pl.pallas_call(
        functools.partial(_ssd_kernel, rows_in=rows_in, rows=rows, zero_init=zero_init),
        grid=(b, l // rows_in),
        in_specs=in_specs,
        out_specs=[pl.BlockSpec((None, rows_in, SSM_INNER), lambda i, j: (i, j, 0)),
                   pl.BlockSpec((None, SSM_HEADS, SSM_P, SSM_N), lambda i, j: (i, 0, 0, 0))],
        out_shape=[jax.ShapeDtypeStruct((b, l, SSM_INNER), F32),
                   jax.ShapeDtypeStruct((b, SSM_HEADS, SSM_P, SSM_N), F32)],
        scratch_shapes=[pltpu.VMEM((8 + rows, SSM_CONV_DIM), F32), pltpu.VMEM((rows, SMALL_W), F32)],
        compiler_params=_cparams(("parallel", "arbitrary")),
        name="ssd",
    )(*args)


def _attend_blocks(q_ref, k_ref, v_ref, acc_ref, lse_ref, dil, blocks):
    span = HEAD
    scale = HEAD ** -0.5
    ii = _iota2((span, span), 0)
    jj = _iota2((span, span), 1)
    cur_ok = jj <= ii
    prev_ok = jj >= ii
    rows = lambda st: pl.ds(st, span) if dil == 1 else pl.ds(st, span, stride=dil)
    q = [q_ref[rows(st), :] for st, _, _ in blocks]
    s_c = [jnp.where(cur_ok, _dot_nt(q[i], k_ref[rows(st), :]) * scale, NEG_INF) for i, (st, _, _) in enumerate(blocks)]
    m = [jnp.max(s, axis=-1, keepdims=True) for s in s_c]
    s_p = []
    for i, (_, sp, ok) in enumerate(blocks):
        if sp is None:
            s_p.append(None)
            continue
        mask = prev_ok if ok is True else jnp.logical_and(prev_ok, ok)
        s_p.append(jnp.where(mask, _dot_nt(q[i], k_ref[rows(sp), :]) * scale, NEG_INF))
        m[i] = jnp.maximum(m[i], jnp.max(s_p[i], axis=-1, keepdims=True))
    for i, (st, sp, _) in enumerate(blocks):
        p_c = jnp.exp(s_c[i] - m[i])
        l = jnp.sum(p_c, axis=-1, keepdims=True)
        acc = _dot(p_c, v_ref[rows(st), :])
        if sp is not None:
            p_p = jnp.exp(s_p[i] - m[i])
            l = l + jnp.sum(p_p, axis=-1, keepdims=True)
            acc = acc + _dot(p_p, v_ref[rows(sp), :])
        acc_ref[rows(st), :] = acc / l
        lse_ref[rows(st), :] = jnp.broadcast_to(m[i] + jnp.log(l), (span, HEAD))


def _dil_prompt_kernel(*refs, seq):
    n_g = len(DIL_GROUPS)
    qs, ks, vs = refs[0:n_g], refs[n_g:2 * n_g], refs[2 * n_g:3 * n_g]
    o_ref, acc_s, lse_s = refs[3 * n_g:]
    u = ATTN_UNROLL
    for g, (_, dil) in enumerate(DIL_GROUPS):
        nb = seq // dil // HEAD
        attend = functools.partial(_attend_blocks, qs[g], ks[g], vs[g], acc_s.at[g], lse_s.at[g], dil)
        if dil == 1:
            assert nb % u == 0

            def body(it, carry, attend=attend):
                blocks = []
                for j in range(u):
                    st = pl.multiple_of((it * u + j) * HEAD, HEAD)
                    prev = pl.multiple_of(jnp.maximum(st - HEAD, 0), HEAD)
                    blocks.append((st, prev, (it > 0) if j == 0 else True))
                attend(blocks)
                return carry
            lax.fori_loop(0, nb // u, body, 0)
        elif nb > 1:
            def body(r, carry, attend=attend, dil=dil, nb=nb):
                attend([(r + n * dil * HEAD, None if n == 0 else r + (n - 1) * dil * HEAD, True) for n in range(nb)])
                return carry
            lax.fori_loop(0, dil, body, 0)
        else:
            assert dil % u == 0

            def body(it, carry, attend=attend):
                attend([(it * u + j, None, True) for j in range(u)])
                return carry
            lax.fori_loop(0, dil // u, body, 0)

    rows_per = 256

    def merge(it, carry):
        rows = pl.ds(pl.multiple_of(it * rows_per, rows_per), rows_per)
        lses = [lse_s[g, rows, :] for g in range(n_g)]
        top = functools.reduce(jnp.maximum, lses)
        wts = [jnp.exp(x - top) for x in lses]
        o_ref[rows, :] = sum(w * acc_s[g, rows, :] for g, w in enumerate(wts)) / sum(wts)
        return carry
    lax.fori_loop(0, seq // rows_per, merge, 0)


def dilated_prompt(proj):
    b, s, _ = proj.shape
    for _, dil in DIL_GROUPS:
        assert s % (dil * HEAD) == 0

    def src(off, g):
        base = off // HEAD + g * DIL_HPG
        return pl.BlockSpec((None, s, HEAD), lambda i, h: (i, 0, base + h))

    n_g = len(DIL_GROUPS)
    in_specs = [src(off - OFF_B, g) for off in (OFF_DILQ, OFF_DILK, OFF_DILV) for g in range(n_g)]
    return pl.pallas_call(
        functools.partial(_dil_prompt_kernel, seq=s),
        grid=(b, DIL_HPG),
        in_specs=in_specs,
        out_specs=pl.BlockSpec((None, s, HEAD), lambda i, h: (i, 0, h)),
        out_shape=jax.ShapeDtypeStruct((b, s, DIL_OUT), F32),
        scratch_shapes=[pltpu.VMEM((n_g, s, HEAD), F32), pltpu.VMEM((n_g, s, HEAD), F32)],
        compiler_params=_cparams(("parallel", "parallel")),
        name="dilated_prompt",
    )(*([proj] * (3 * n_g)))


def _dil_sample_kernel(new_ref, *rest, t_new):
    caches, o_ref = rest[:-1], rest[-1]
    nh = DIL_HPG
    nq = t_new * nh
    h_shift = int(math.log2(nh))
    n_g = len(DIL_GROUPS)
    scale = HEAD ** -0.5
    rounds = []
    for g, (_, dil) in enumerate(DIL_GROUPS):
        kc, vc = caches[2 * g], caches[2 * g + 1]
        if dil == 1:
            rounds.append((g, 0, nq, nh, dil, kc[...], vc[...]))
        else:
            rounds += [(g, 8 * r, 8, 8, dil, kc[:, 8 * r:8 * r + 8, :].reshape(kc.shape[0] * 8, HEAD),
                        vc[:, 8 * r:8 * r + 8, :].reshape(vc.shape[0] * 8, HEAD)) for r in range(nq // 8)]
    q = [new_ref[g][row0:row0 + nr, :] for g, row0, nr, _, _, _, _ in rounds]
    s_c, s_n = [], []
    for i, (g, row0, nr, width, dil, k2d, _) in enumerate(rounds):
        n_keys = k2d.shape[0]
        qi = _iota2((nr, n_keys), 0) + row0
        kj = _iota2((nr, n_keys), 1)
        same_slot = (kj & (width - 1)) == (qi & (width - 1))
        first = (qi >> h_shift) >> int(math.log2(dil))
        ok = jnp.logical_and(same_slot, (kj >> int(math.log2(width))) >= first)
        s_c.append(jnp.where(ok, _dot_nt(q[i], k2d) * scale, NEG_INF))
        qn = _iota2((nr, nq), 0) + row0
        kn = _iota2((nr, nq), 1)
        gap = (qn >> h_shift) - (kn >> h_shift)
        ok_n = jnp.logical_and((kn & (nh - 1)) == (qn & (nh - 1)),
                               jnp.logical_and(gap >= 0, (gap & (dil - 1)) == 0))
        s_n.append(jnp.where(ok_n, _dot_nt(q[i], new_ref[n_g + g]) * scale, NEG_INF))
    m = [jnp.maximum(jnp.max(a, axis=-1, keepdims=True), jnp.max(b, axis=-1, keepdims=True)) for a, b in zip(s_c, s_n)]
    p_c = [jnp.exp(a - mm) for a, mm in zip(s_c, m)]
    p_n = [jnp.exp(a - mm) for a, mm in zip(s_n, m)]
    l = [jnp.sum(a, axis=-1, keepdims=True) + jnp.sum(b, axis=-1, keepdims=True) for a, b in zip(p_c, p_n)]
    o = [(_dot(p_c[i], r[6]) + _dot(p_n[i], new_ref[2 * n_g + r[0]])) / l[i] for i, r in enumerate(rounds)]
    lse = [mm + jnp.log(ll) for mm, ll in zip(m, l)]
    outs, lses = [], []
    for g in range(n_g):
        mine = [i for i, r in enumerate(rounds) if r[0] == g]
        outs.append(o[mine[0]] if len(mine) == 1 else jnp.concatenate([o[i] for i in mine], axis=0))
        lses.append(lse[mine[0]] if len(mine) == 1 else jnp.concatenate([lse[i] for i in mine], axis=0))
    top = functools.reduce(jnp.maximum, lses)
    wts = [jnp.exp(x - top) for x in lses]
    o_ref[...] = sum(w * o for w, o in zip(wts, outs)) / sum(wts)


def dilated_sample(proj, layer, caches):
    b, t_new, _ = proj.shape
    nh = DIL_HPG
    assert nh == 4 and (t_new * nh) % 8 == 0
    views, specs = [], []
    for g, (window, dil) in enumerate(DIL_GROUPS):
        for c in caches[2 * g:2 * g + 2]:
            lb = c.shape[2]
            assert lb == window and lb % dil == 0 and (dil == 1 or t_new <= dil) and dil & (dil - 1) == 0
            if dil == 1:
                views.append(c.reshape(c.shape[0], b, lb * nh, HEAD))
                specs.append(pl.BlockSpec((None, None, lb * nh, HEAD), lambda i: (layer, i, 0, 0)))
            else:
                views.append(c.reshape(c.shape[0], b, lb // dil, dil * nh, HEAD))
                specs.append(pl.BlockSpec((None, None, lb // dil, t_new * nh, HEAD), lambda i: (layer, i, 0, 0, 0)))
    n_g = len(DIL_GROUPS)
    new = proj[:, :, OFF_DILQ - OFF_B:OFF_DILQ - OFF_B + 3 * DIL_WIDTH].reshape(b, t_new, 3, n_g, nh, HEAD)
    new = new.transpose(0, 2, 3, 1, 4, 5).reshape(b, 3 * n_g, t_new * nh, HEAD)
    out = pl.pallas_call(
        functools.partial(_dil_sample_kernel, t_new=t_new),
        grid=(b,),
        in_specs=[pl.BlockSpec((None, 3 * n_g, t_new * nh, HEAD), lambda i: (i, 0, 0, 0))] + specs,
        out_specs=pl.BlockSpec((None, t_new * nh, HEAD), lambda i: (i, 0, 0)),
        out_shape=jax.ShapeDtypeStruct((b, t_new * nh, HEAD), F32),
        compiler_params=_cparams(("parallel",)),
        name="dilated_sample",
    )(new, *views)
    return out.reshape(b, t_new, DIL_OUT)


def _merge_kernel(x_ref, odn_ref, odil_ref, ossm_ref, g_dn, g_dil, g_ssm, w_dn, w_dil, w_ssm, w_out, gam_ref, o_ref):
    merged = (_sigmoid(g_dn[...].astype(F32)) * _dot(odn_ref[...], w_dn[...])
              + _sigmoid(g_dil[...].astype(F32)) * _dot(odil_ref[...], w_dil[...])
              + _sigmoid(g_ssm[...].astype(F32)) * _dot(ossm_ref[...], w_ssm[...]))
    y = _dot(merged, w_out[...])
    o_ref[...] = x_ref[...] + _rms_scale(y) * gam_ref[...]


def branch_merge(x, o_dn, o_dil, o_ssm, proj, w_dn, w_dil, w_ssm, w_out, gam, *, tm):
    n = x.shape[0]
    row = lambda w: pl.BlockSpec((tm, w), lambda i: (i, 0))
    gate = lambda k: pl.BlockSpec((tm, D_MODEL), lambda i: (i, OFF_GATES // D_MODEL + k))
    const = lambda a: pl.BlockSpec(a.shape, lambda i: (0,) * a.ndim)
    gam = gam.reshape(1, D_MODEL)
    return pl.pallas_call(
        _merge_kernel,
        grid=(n // tm,),
        in_specs=[row(D_MODEL), row(DN_WIDTH), row(DIL_OUT), row(SSM_INNER), gate(0), gate(1), gate(2),
                  const(w_dn), const(w_dil), const(w_ssm), const(w_out), const(gam)],
        out_specs=row(D_MODEL),
        out_shape=jax.ShapeDtypeStruct((n, D_MODEL), F32),
        compiler_params=_cparams(("parallel",)),
        name="branch_merge",
    )(x, o_dn, o_dil, o_ssm, proj, proj, proj, w_dn, w_dil, w_ssm, w_out, gam)


def _mem_attn_kernel(q_ref, k_ref, v_ref, o_ref):
    scale = HEAD ** -0.5
    for h in range(MEM_HEADS):
        cols = slice(h * HEAD, (h + 1) * HEAD)
        s = _dot_nt(q_ref[:, cols], k_ref[:, cols]) * scale
        m = jnp.max(s, axis=-1, keepdims=True)
        p = jnp.exp(s - m)
        o_ref[:, cols] = _dot(p, v_ref[:, cols]) / jnp.sum(p, axis=-1, keepdims=True)


def mem_attention(q, kv, *, tq):
    b, l, _ = q.shape
    n_mem = kv.shape[1]
    return pl.pallas_call(
        _mem_attn_kernel,
        grid=(b, l // tq),
        in_specs=[pl.BlockSpec((None, tq, MEM_WIDTH), lambda i, j: (i, j, 0)),
                  pl.BlockSpec((None, n_mem, MEM_WIDTH), lambda i, j: (i, 0, 0)),
                  pl.BlockSpec((None, n_mem, MEM_WIDTH), lambda i, j: (i, 0, 1))],
        out_specs=pl.BlockSpec((None, tq, MEM_WIDTH), lambda i, j: (i, j, 0)),
        out_shape=jax.ShapeDtypeStruct((b, l, MEM_WIDTH), F32),
        compiler_params=_cparams(("parallel", "parallel")),
        name="mem_attention",
    )(q, kv, kv)


def _mem_attn_slot_kernel(q_ref, k_ref, v_ref, o_ref):
    scale = HEAD ** -0.5
    nq, n_keys = q_ref.shape[0], k_ref.shape[0]
    same_head = (_iota2((nq, n_keys), 1) & (MEM_HEADS - 1)) == (_iota2((nq, n_keys), 0) & (MEM_HEADS - 1))
    s = jnp.where(same_head, _dot_nt(q_ref[...], k_ref[...]) * scale, NEG_INF)
    m = jnp.max(s, axis=-1, keepdims=True)
    p = jnp.exp(s - m)
    o_ref[...] = _dot(p, v_ref[...]) / jnp.sum(p, axis=-1, keepdims=True)


def mem_attention_cached(q, layer, cache_k, cache_v):
    b, l, _ = q.shape
    n_mem = cache_k.shape[2]
    assert MEM_HEADS & (MEM_HEADS - 1) == 0 and (l * MEM_HEADS) % 8 == 0
    rows = lambda a: a.reshape(a.shape[0], b, n_mem * MEM_HEADS, HEAD)
    kv_spec = pl.BlockSpec((None, None, n_mem * MEM_HEADS, HEAD), lambda i: (layer, i, 0, 0))
    out = pl.pallas_call(
        _mem_attn_slot_kernel,
        grid=(b,),
        in_specs=[pl.BlockSpec((None, l * MEM_HEADS, HEAD), lambda i: (i, 0, 0)), kv_spec, kv_spec],
        out_specs=pl.BlockSpec((None, l * MEM_HEADS, HEAD), lambda i: (i, 0, 0)),
        out_shape=jax.ShapeDtypeStruct((b, l * MEM_HEADS, HEAD), F32),
        compiler_params=_cparams(("parallel",)),
        name="mem_attention_cached",
    )(q.reshape(b, l * MEM_HEADS, HEAD), rows(cache_k), rows(cache_v))
    return out.reshape(b, l, MEM_WIDTH)


def _matmul_norm_res_kernel(x_ref, a_ref, w_ref, gam_ref, o_ref):
    y = _dot(a_ref[...], w_ref[...])
    o_ref[...] = x_ref[...] + _rms_scale(y) * gam_ref[...]


def matmul_norm_residual(x, a, w, gam, *, tm):
    n, k = a.shape
    return pl.pallas_call(
        _matmul_norm_res_kernel,
        grid=(n // tm,),
        in_specs=[pl.BlockSpec((tm, D_MODEL), lambda i: (i, 0)), pl.BlockSpec((tm, k), lambda i: (i, 0)),
                  pl.BlockSpec((k, D_MODEL), lambda i: (0, 0)), pl.BlockSpec((1, D_MODEL), lambda i: (0, 0))],
        out_specs=pl.BlockSpec((tm, D_MODEL), lambda i: (i, 0)),
        out_shape=jax.ShapeDtypeStruct((n, D_MODEL), F32),
        compiler_params=_cparams(("parallel",)),
        name="matmul_norm_residual",
    )(x, a, w, gam.reshape(1, D_MODEL))


def _mlp_kernel(x_ref, gpre_ref, w1_ref, w2_ref, gpost_ref, o_ref, h_ref, acc_ref):
    k = pl.program_id(1)

    @pl.when(k == 0)
    def _():
        h_ref[...] = (_rms_scale(x_ref[...]) * gpre_ref[...]).astype(BF16)
        acc_ref[...] = jnp.zeros(acc_ref.shape, F32)

    f = jnp.maximum(jnp.dot(h_ref[...], w1_ref[...], preferred_element_type=F32), 0.0)
    acc_ref[...] += _dot(f * f, w2_ref[...])

    @pl.when(k == pl.num_programs(1) - 1)
    def _():
        o_ref[...] = x_ref[...] + _rms_scale(acc_ref[...]) * gpost_ref[...]


def mlp(x, gpre, w1, w2, gpost, *, tm, tk):
    n = x.shape[0]
    return pl.pallas_call(
        _mlp_kernel,
        grid=(n // tm, D_FF // tk),
        in_specs=[pl.BlockSpec((tm, D_MODEL), lambda i, k: (i, 0)), pl.BlockSpec((1, D_MODEL), lambda i, k: (0, 0)),
                  pl.BlockSpec((D_MODEL, tk), lambda i, k: (0, k)), pl.BlockSpec((tk, D_MODEL), lambda i, k: (k, 0)),
                  pl.BlockSpec((1, D_MODEL), lambda i, k: (0, 0))],
        out_specs=pl.BlockSpec((tm, D_MODEL), lambda i, k: (i, 0)),
        out_shape=jax.ShapeDtypeStruct((n, D_MODEL), F32),
        scratch_shapes=[pltpu.VMEM((tm, D_MODEL), BF16), pltpu.VMEM((tm, D_MODEL), F32)],
        compiler_params=_cparams(("parallel", "arbitrary")),
        name="mlp",
    )(x, gpre.reshape(1, D_MODEL), w1, w2, gpost.reshape(1, D_MODEL))


def reorder_w_in(w_in):
    idx = []
    acc = 0
    for s in IN_SIZES:
        idx.append((acc, acc + s))
        acc += s
    dn_qkv, dn_z, dn_b, dn_a, dil_qkv, ssm_z, ssm_xbc, ssm_dt, gates = [w_in[:, a:b] for a, b in idx]
    small = jnp.concatenate([dn_b, dn_a, ssm_dt], axis=1)
    pad = jnp.zeros((w_in.shape[0], NP - OFF_SMALL - small.shape[1]), w_in.dtype)
    w_a = jnp.concatenate([dn_qkv, dn_z, ssm_xbc, ssm_z, gates], axis=1).astype(BF16)
    w_b = jnp.concatenate([dil_qkv, small, pad], axis=1).astype(BF16)
    return w_a, w_b


def trunk_layer(x, layer, rope, mem, states, p, cfg):
    b, l, _ = x.shape
    n = b * l
    tm = cfg["tm"]
    xf = x.reshape(n, D_MODEL)
    w_a, w_b = p["w_in"]
    proj = norm_matmul(xf, p["norm_mix_pre"], w_a, tm=cfg["tm_in"], tn=1024, out_dtype=cfg["proj_dtype"])
    proj_b = norm_matmul(xf, p["norm_mix_pre"], w_b, tm=cfg["tm_in"], tn=1024,
                         rope=rope + ((OFF_DILQ - OFF_B) // 1024, (OFF_DILV - OFF_B) // 1024))
    proj3 = proj.reshape(b, l, OFF_B)
    proj_b3 = proj_b.reshape(b, l, NP - OFF_B)
    dn_conv, dn_state, ssm_conv, ssm_state = states[:4] if states is not None else (None,) * 4

    o_dn, dn_state_new = deltanet(proj3, proj_b3, layer, dn_conv, dn_state, p["dn_conv_w"], p["dn_a_log"],
                                  p["dn_dt_bias"], p["dn_norm"], rows_in=cfg["rows_in"], rows=cfg["rows"])
    o_ssm, ssm_state_new = ssd(proj3, proj_b3, layer, ssm_conv, ssm_state, p["ssm_conv_w"], p["ssm_conv_b"],
                               p["ssm_a_log"], p["ssm_dt_bias"], p["ssm_d"], p["ssm_norm"],
                               rows_in=cfg["rows_in_ssd"], rows=cfg["rows_ssd"])
    win_new = []
    o_dil = dilated_prompt(proj_b3) if states is None else dilated_sample(proj_b3, layer, states[4:])
    for gi, (window, _) in enumerate(DIL_GROUPS):
        keep = min(window, l) if states is None else l
        for off in (OFF_DILK, OFF_DILV):
            c0 = off - OFF_B + gi * DIL_OUT
            win_new.append(proj_b3[:, l - keep:, c0:c0 + DIL_OUT].reshape(b, keep, DIL_HPG, HEAD))

    xf = branch_merge(xf, o_dn.reshape(n, DN_WIDTH), o_dil.reshape(n, DIL_OUT), o_ssm.reshape(n, SSM_INNER), proj,
                      p["w_br_dn"], p["w_br_dil"], p["w_br_ssm"], p["w_out"], p["norm_mix_post"], tm=cfg["tm_merge"])

    qm = norm_matmul(xf, p["norm_mem_pre"], p["w_mq"], tm=tm, tn=MEM_WIDTH).reshape(b, l, MEM_WIDTH)
    om = mem_attention(qm, mem, tq=cfg["tq"]) if states is None else mem_attention_cached(qm, layer, *mem)
    xf = matmul_norm_residual(xf, om.reshape(n, MEM_WIDTH), p["w_mo"], p["norm_mem_post"], tm=tm)

    xf = mlp(xf, p["norm_ffn_pre"], p["w_ff1"], p["w_ff2"], p["norm_ffn_post"], tm=tm, tk=1024)

    def conv_tail(buf, off, width):
        new = proj3[:, :, off:off + width].astype(F32)
        if l < CONV_W - 1:
            prev = jnp.zeros((b, CONV_W - 1, width), F32) if buf is None else buf[layer]
            new = jnp.concatenate([prev, new], axis=1)
        return new[:, new.shape[1] - (CONV_W - 1):]

    new_states = (conv_tail(dn_conv, OFF_DNQKV, 3 * DN_WIDTH), dn_state_new,
                  conv_tail(ssm_conv, OFF_XBC, SSM_CONV_DIM), ssm_state_new) + tuple(win_new)
    return xf.reshape(b, l, D_MODEL), new_states


def kernel(x_prompt, x_sample, state_dn_conv, state_dn, state_ssm_conv, state_ssm, cache_win1_k, cache_win1_v, cache_win2_k, cache_win2_v, cache_win3_k, cache_win3_v, cache_mem_k, cache_mem_v, mem_prompt, norm_mix_pre, w_in, dn_conv_w, dn_a_log, dn_dt_bias, dn_norm, ssm_conv_w, ssm_conv_b, ssm_a_log, ssm_dt_bias, ssm_d, ssm_norm, w_br_dn, w_br_dil, w_br_ssm, w_out, norm_mix_post, norm_mem_pre, norm_mem_kv, w_mq, w_mkv, w_mo, norm_mem_post, norm_ffn_pre, w_ff1, w_ff2, norm_ffn_post):
    n_p, s = x_prompt.shape[:2]
    n_s, t = x_sample.shape[:2]
    n_mem = mem_prompt.shape[1]
    depth = w_in.shape[0]
    tm_p, tm_s = 1024, n_s * t
    rope_p = rope_tables(jnp.arange(s, dtype=F32))
    rope_s = rope_tables(jnp.tile(PAST_LEN + jnp.arange(t, dtype=F32), tm_s // t))
    cfg_p = dict(tm=tm_p, tm_in=2 * tm_p, tm_merge=256, rows_in=64, rows=64, rows_in_ssd=128, rows_ssd=128, tq=512,
                 proj_dtype=BF16)
    cfg_s = dict(tm=tm_s, tm_in=tm_s, tm_merge=tm_s, rows_in=t, rows=8, rows_in_ssd=t, rows_ssd=8, tq=t,
                 proj_dtype=F32)
    states_s = (state_dn_conv, state_dn, state_ssm_conv, state_ssm, cache_win1_k, cache_win1_v, cache_win2_k,
                cache_win2_v, cache_win3_k, cache_win3_v)
    xp, xs = x_prompt, x_sample
    new_p = [[] for _ in range(12)]
    new_s = [[] for _ in range(10)]
    for l in range(depth):
        bf = lambda a: a[l].astype(BF16)
        prm = dict(norm_mix_pre=norm_mix_pre[l], w_in=reorder_w_in(w_in[l]), dn_conv_w=dn_conv_w[l],
                   dn_a_log=dn_a_log[l], dn_dt_bias=dn_dt_bias[l], dn_norm=dn_norm[l], ssm_conv_w=ssm_conv_w[l],
                   ssm_conv_b=ssm_conv_b[l], ssm_a_log=ssm_a_log[l], ssm_dt_bias=ssm_dt_bias[l], ssm_d=ssm_d[l],
                   ssm_norm=ssm_norm[l], w_br_dn=bf(w_br_dn), w_br_dil=bf(w_br_dil), w_br_ssm=bf(w_br_ssm),
                   w_out=bf(w_out), norm_mix_post=norm_mix_post[l], norm_mem_pre=norm_mem_pre[l], w_mq=bf(w_mq),
                   w_mo=bf(w_mo), norm_mem_post=norm_mem_post[l], norm_ffn_pre=norm_ffn_pre[l], w_ff1=bf(w_ff1),
                   w_ff2=bf(w_ff2), norm_ffn_post=norm_ffn_post[l])
        mkv = norm_matmul(mem_prompt.reshape(n_p * n_mem, D_MODEL), norm_mem_kv[l], bf(w_mkv),
                          tm=min(1024, n_p * n_mem), tn=1024)
        mkv = mkv.reshape(n_p, n_mem, 2 * MEM_WIDTH)
        mk = mkv[:, :, :MEM_WIDTH].reshape(n_p, n_mem, MEM_HEADS, HEAD)
        mv = mkv[:, :, MEM_WIDTH:].reshape(n_p, n_mem, MEM_HEADS, HEAD)
        xp, st_p = trunk_layer(xp, l, rope_p, mkv, None, prm, cfg_p)
        for i, a in enumerate(st_p + (mk, mv)):
            new_p[i].append(a)
        xs, st_s = trunk_layer(xs, l, rope_s, (cache_mem_k, cache_mem_v), states_s, prm, cfg_s)
        for i, a in enumerate(st_s):
            new_s[i].append(a)
    outs_p = [jnp.stack(a) for a in new_p]
    outs_s = [jnp.stack(a) for a in new_s]
    return (xp, xs, *outs_p, *outs_s)
```

```python
import functools
import math

import jax
import jax.numpy as jnp
from jax import lax
from jax.experimental import pallas as pl
from jax.experimental.pallas import tpu as pltpu

F32 = jnp.float32
BF16 = jnp.bfloat16

D_MODEL = 1024
CONV_W = 4
DN_HEADS = 8
DN_DK = 128
DN_WIDTH = DN_HEADS * DN_DK
DIL_GROUPS = ((128, 1), (512, 4), (2048, 16))
DIL_HPG = 4
HEAD = 128
DIL_HEADS = DIL_HPG * len(DIL_GROUPS)
DIL_WIDTH = DIL_HEADS * HEAD
DIL_OUT = DIL_HPG * HEAD
ROT_DIM = HEAD // 4
ROPE_THETA = 500000.0
PAST_LEN = 2048
SSM_HEADS = 16
SSM_P = 64
SSM_N = 128
SSM_GROUPS = 4
SSM_HPG = SSM_HEADS // SSM_GROUPS
SSM_INNER = SSM_HEADS * SSM_P
SSM_CONV_DIM = SSM_INNER + 2 * SSM_GROUPS * SSM_N
MEM_HEADS = 4
MEM_WIDTH = MEM_HEADS * HEAD
D_FF = 4 * D_MODEL
EPS = 1e-6
NEG_INF = -1e30
IN_SIZES = (3 * DN_WIDTH, DN_WIDTH, DN_HEADS, DN_HEADS, 3 * DIL_WIDTH, SSM_INNER, SSM_CONV_DIM, SSM_HEADS,
            3 * D_MODEL)

OFF_DNQKV = 0
OFF_DNZ = OFF_DNQKV + 3 * DN_WIDTH
OFF_XBC = OFF_DNZ + DN_WIDTH
OFF_SSMZ = OFF_XBC + SSM_CONV_DIM
OFF_GATES = OFF_SSMZ + SSM_INNER
OFF_DILQ = OFF_GATES + 3 * D_MODEL
OFF_DILK = OFF_DILQ + DIL_WIDTH
OFF_DILV = OFF_DILK + DIL_WIDTH
OFF_SMALL = OFF_DILV + DIL_WIDTH
SMALL_W = 128
NP = 15 * 1024
OFF_B = OFF_DILQ
LANE_DN_B = 0
LANE_DN_A = DN_HEADS
LANE_SSM_DT = 2 * DN_HEADS

SOLVE_BASE = 16
ATTN_UNROLL = 4

VMEM_LIMIT = 56 * 1024 * 1024


def _cparams(sem):
    return pltpu.CompilerParams(dimension_semantics=sem, vmem_limit_bytes=VMEM_LIMIT)


def _dot(a, b):
    return jnp.dot(a.astype(BF16), b.astype(BF16), preferred_element_type=F32)


def _dot_nt(a, b):
    return lax.dot_general(a.astype(BF16), b.astype(BF16), (((1,), (1,)), ((), ())), preferred_element_type=F32)


def _dot_tn(a, b):
    return lax.dot_general(a.astype(BF16), b.astype(BF16), (((0,), (0,)), ((), ())), preferred_element_type=F32)


def _split3(x):
    hi = x.astype(BF16)
    r = x - hi.astype(F32)
    mid = r.astype(BF16)
    lo = (r - mid.astype(F32)).astype(BF16)
    return hi, mid, lo


def _sel_dot(sel, x):
    return sum(jnp.dot(sel, p, preferred_element_type=F32) for p in _split3(x))


def _sel_dot_nt(sel, x):
    return sum(lax.dot_general(sel, p, (((1,), (1,)), ((), ())), preferred_element_type=F32) for p in _split3(x))


def _rms_scale(x):
    return x * lax.rsqrt(jnp.mean(x * x, axis=-1, keepdims=True) + EPS)


def _softplus(x):
    return jnp.maximum(x, 0.0) + jnp.log1p(jnp.exp(-jnp.abs(x)))


def _sigmoid(x):
    return 1.0 / (1.0 + jnp.exp(-x))


def _silu(x):
    return x * _sigmoid(x)


def _iota2(shape, axis):
    return lax.broadcasted_iota(jnp.int32, shape, axis)


def _norm_matmul_kernel(*refs, rope_tiles, tn):
    if rope_tiles is None:
        x_ref, g_ref, w_ref, o_ref, h_ref = refs
    else:
        x_ref, g_ref, w_ref, cos_ref, sa_ref, sb_ref, o_ref, h_ref = refs
    j = pl.program_id(1)

    @pl.when(j == 0)
    def _():
        h_ref[...] = (_rms_scale(x_ref[...]) * g_ref[...]).astype(BF16)

    y = jnp.dot(h_ref[...], w_ref[...], preferred_element_type=F32)
    if rope_tiles is None:
        o_ref[...] = y.astype(o_ref.dtype)
        return
    lo, hi = rope_tiles
    is_rope = jnp.logical_and(j >= lo, j < hi)

    @pl.when(is_rope)
    def _():
        cos, sa, sb = cos_ref[...], sa_ref[...], sb_ref[...]
        for c in range(tn // HEAD):
            t = y[:, c * HEAD:(c + 1) * HEAD]
            o_ref[:, c * HEAD:(c + 1) * HEAD] = (
                t * cos + pltpu.roll(t, HEAD - ROT_DIM // 2, 1) * sa + pltpu.roll(t, ROT_DIM // 2, 1) * sb)

    @pl.when(jnp.logical_not(is_rope))
    def _():
        o_ref[...] = y


def norm_matmul(x, g, w, *, tm, tn, rope=None, out_dtype=F32):
    n, k = x.shape
    m = w.shape[1]
    assert n % tm == 0 and m % tn == 0
    in_specs = [pl.BlockSpec((tm, k), lambda i, j: (i, 0)),
                pl.BlockSpec((1, k), lambda i, j: (0, 0)),
                pl.BlockSpec((k, tn), lambda i, j: (0, j))]
    args = [x, g.reshape(1, k), w]
    rope_tiles = None
    if rope is not None:
        cos, sa, sb, lo, hi = rope
        period = cos.shape[0] // tm
        assert cos.shape[0] % tm == 0
        tab = pl.BlockSpec((tm, HEAD), lambda i, j: (i % period, 0))
        in_specs += [tab, tab, tab]
        args += [cos, sa, sb]
        rope_tiles = (lo, hi)
    return pl.pallas_call(
        functools.partial(_norm_matmul_kernel, rope_tiles=rope_tiles, tn=tn),
        grid=(n // tm, m // tn),
        in_specs=in_specs,
        out_specs=pl.BlockSpec((tm, tn), lambda i, j: (i, j)),
        out_shape=jax.ShapeDtypeStruct((n, m), out_dtype),
        scratch_shapes=[pltpu.VMEM((tm, k), BF16)],
        compiler_params=_cparams(("parallel", "arbitrary")),
        name="norm_matmul",
    )(*args)


def rope_tables(pos):
    half = ROT_DIM // 2
    inv = jnp.power(ROPE_THETA, -jnp.arange(half, dtype=F32) * 2.0 / ROT_DIM)
    ang = pos[:, None] * inv[None, :]
    cos, sin = jnp.cos(ang), jnp.sin(ang)
    n = pos.shape[0]
    ones = jnp.ones((n, HEAD - ROT_DIM), F32)
    zeros = jnp.zeros((n, HEAD - ROT_DIM), F32)
    z16 = jnp.zeros((n, half), F32)
    return (jnp.concatenate([cos, cos, ones], axis=1),
            jnp.concatenate([-sin, z16, zeros], axis=1),
            jnp.concatenate([z16, sin, zeros], axis=1))


def _conv_cols(xbuf, cw_ref, c0, width, rows, bias_ref=None):
    acc = xbuf[5:5 + rows, c0:c0 + width] * cw_ref[0:1, c0:c0 + width]
    for j in range(1, CONV_W):
        acc = acc + xbuf[5 + j:5 + j + rows, c0:c0 + width] * cw_ref[j:j + 1, c0:c0 + width]
    if bias_ref is not None:
        acc = acc + bias_ref[0:1, c0:c0 + width]
    return _silu(acc)


def _stage_rows(xbuf, smbuf, parts, sm_ref, cs_ref, rows_in, rows):
    @pl.when(pl.program_id(1) == 0)
    def _():
        xbuf[5:8, :] = jnp.zeros((CONV_W - 1, xbuf.shape[1]), F32) if cs_ref is None else cs_ref[...]

    @pl.when(pl.program_id(1) > 0)
    def _():
        xbuf[5:8, :] = xbuf[5 + rows_in:8 + rows_in, :]

    for ref, c0 in parts:
        xbuf[8:8 + rows_in, c0:c0 + ref.shape[1]] = ref[...].astype(F32)
    if rows_in < rows:
        xbuf[8 + rows_in:8 + rows, :] = jnp.zeros((rows - rows_in, xbuf.shape[1]), F32)
        smbuf[...] = jnp.zeros(smbuf.shape, F32)
        smbuf[0:rows_in, :] = sm_ref[...]
        return smbuf[...]
    return sm_ref[...]


def _lane_vec(v, lane0):
    return jnp.zeros((1, SMALL_W), F32).at[0, lane0:lane0 + v.shape[0]].set(v)


def _state_specs(layer, conv_state, state):
    cs = pl.BlockSpec((None, None) + conv_state.shape[2:], lambda i, j: (layer, i, 0, 0))
    st = pl.BlockSpec((None, None) + state.shape[2:], lambda i, j: (layer, i, 0, 0, 0))
    return cs, st


def _carry_earlier_layers(st_out, prev_ref, n_prev):
    if not n_prev:
        return st_out

    @pl.when(pl.program_id(1) == 0)
    def _():
        st_out[0:n_prev] = prev_ref[...]

    return st_out.at[n_prev]


def _dn_kernel(*refs, rows_in, rows, zero_init, n_prev):
    n_in = 9 + (0 if zero_init else 2) + (1 if n_prev else 0)
    ins, (o_ref, st_out, xbuf, smbuf) = refs[:n_in], refs[n_in:]
    q_ref, k_ref, v_ref, z_ref, sm_ref, cw_ref, alog_ref, dtb_ref, gam_ref = ins[:9]
    cs_ref, s0_ref = (None, None) if zero_init else ins[9:11]
    st_ref = _carry_earlier_layers(st_out, ins[-1] if n_prev else None, n_prev)
    c = rows
    heads = range(DN_HEADS)

    @pl.when(pl.program_id(1) == 0)
    def _():
        st_ref[...] = jnp.zeros(st_ref.shape, F32) if zero_init else s0_ref[...]

    sm = _stage_rows(xbuf, smbuf, ((q_ref, 0), (k_ref, DN_WIDTH), (v_ref, 2 * DN_WIDTH)), sm_ref, cs_ref,
                     rows_in, rows)
    beta_all = _sigmoid(sm)
    g_all = -jnp.exp(alog_ref[...]) * _softplus(sm + dtb_ref[...])
    if rows_in < rows:
        live = (_iota2((c, 1), 0) < rows_in).astype(F32)
        beta_all = beta_all * live
        g_all = g_all * live
    ii = _iota2((c, c), 0)
    jj = _iota2((c, c), 1)
    tri = ii >= jj
    strict = ii > jj
    eye = (ii == jj).astype(F32)
    gcum_all = _sel_dot(tri.astype(BF16), g_all)
    sel = (_iota2((DN_HEADS, SMALL_W), 1) == _iota2((DN_HEADS, SMALL_W), 0) + LANE_DN_A).astype(BF16)
    grow_all = _sel_dot_nt(sel, gcum_all)
    base = min(c, SOLVE_BASE)
    n_sq = int(math.log2(base)) - 1
    n_lvl = int(math.log2(c // base))
    assert 2 ** (n_sq + 1) == base and base * 2 ** n_lvl == c
    same_blk = [(ii >> (n_sq + 1 + lvl)) == (jj >> (n_sq + 1 + lvl)) for lvl in range(n_lvl + 1)]

    q = [_conv_cols(xbuf, cw_ref, h * DN_DK, DN_DK, c) for h in heads]
    k = [_conv_cols(xbuf, cw_ref, DN_WIDTH + h * DN_DK, DN_DK, c) for h in heads]
    v = [_conv_cols(xbuf, cw_ref, 2 * DN_WIDTH + h * DN_DK, DN_DK, c) for h in heads]
    q = [x * lax.rsqrt(jnp.sum(x * x, axis=-1, keepdims=True) + EPS) * (DN_DK ** -0.5) for x in q]
    k = [x * lax.rsqrt(jnp.sum(x * x, axis=-1, keepdims=True) + EPS) for x in k]
    gc = [gcum_all[:, LANE_DN_A + h:LANE_DN_A + h + 1] for h in heads]
    gr = [grow_all[h:h + 1, :] for h in heads]
    beta = [beta_all[:, LANE_DN_B + h:LANE_DN_B + h + 1] for h in heads]
    decay = [jnp.where(tri, jnp.exp(jnp.where(tri, gc[h] - gr[h], 0.0)), 0.0) for h in heads]
    eg = [jnp.exp(gc[h]) for h in heads]
    kb = [k[h] * beta[h] for h in heads]
    a_mat = [_dot_nt(kb[h], k[h]) * jnp.where(strict, decay[h], 0.0) for h in heads]
    qk = [_dot_nt(q[h], k[h]) * decay[h] for h in heads]
    x = [-jnp.where(same_blk[0], a_mat[h], 0.0) for h in heads]
    t_inv = [eye + x[h] for h in heads]
    for _ in range(n_sq):
        x = [_dot(x[h], x[h]) for h in heads]
        t_inv = [t_inv[h] + _dot(t_inv[h], x[h]) for h in heads]
    for lvl in range(1, len(same_blk)):
        ring = jnp.logical_and(same_blk[lvl], jnp.logical_not(same_blk[lvl - 1]))
        left = [_dot(t_inv[h], jnp.where(ring, a_mat[h], 0.0)) for h in heads]
        t_inv = [t_inv[h] - _dot(left[h], t_inv[h]) for h in heads]
    sol = [_dot(t_inv[h], jnp.concatenate([v[h] * beta[h], kb[h] * eg[h]], axis=1)) for h in heads]
    s_prev = [st_ref[h] for h in heads]
    v_new = [sol[h][:, :DN_DK] - _dot(sol[h][:, DN_DK:], s_prev[h]) for h in heads]
    o = [_dot(q[h] * eg[h], s_prev[h]) + _dot(qk[h], v_new[h]) for h in heads]
    g_last = [gc[h][c - 1:c, :] for h in heads]
    k_dec = [k[h] * jnp.exp(g_last[h] - gc[h]) for h in heads]
    s_new = [s_prev[h] * jnp.exp(g_last[h]) + _dot_tn(k_dec[h], v_new[h]) for h in heads]
    for h in heads:
        st_ref[h] = s_new[h]
        z = z_ref[:, h * DN_DK:(h + 1) * DN_DK].astype(F32)
        o_ref[:, h * DN_DK:(h + 1) * DN_DK] = _rms_scale(o[h][0:rows_in, :]) * gam_ref[...] * _silu(z)


def _stacked_state_out(prev_states, shape, in_specs, args):
    nd = len(shape)
    if prev_states is None:
        return (pl.BlockSpec((None,) + shape[1:], lambda i, j: (i,) + (0,) * (nd - 1)),
                jax.ShapeDtypeStruct(shape, F32), 0)
    n_prev = prev_states.shape[0]
    in_specs.append(pl.BlockSpec((n_prev, None) + shape[1:], lambda i, j: (0, i) + (0,) * (nd - 1)))
    args.append(prev_states)
    return (pl.BlockSpec((n_prev + 1, None) + shape[1:], lambda i, j: (0, i) + (0,) * (nd - 1)),
            jax.ShapeDtypeStruct((n_prev + 1,) + shape, F32), n_prev)


def deltanet(proj, proj_b, layer, conv_state, state, conv_w, a_log, dt_bias, norm_g, *, rows_in, rows,
             prev_states=None):
    b, l, _ = proj.shape
    assert l % rows_in == 0 and rows_in <= rows
    zero_init = state is None
    blk = lambda w, off: pl.BlockSpec((None, rows_in, w), lambda i, j: (i, j, off // w))
    const = lambda shape: pl.BlockSpec(shape, lambda i, j: (0,) * len(shape))
    in_specs = [blk(DN_WIDTH, OFF_DNQKV), blk(DN_WIDTH, OFF_DNQKV + DN_WIDTH), blk(DN_WIDTH, OFF_DNQKV + 2 * DN_WIDTH),
                blk(DN_WIDTH, OFF_DNZ), blk(SMALL_W, OFF_SMALL - OFF_B),
                const((CONV_W, 3 * DN_WIDTH)), const((1, SMALL_W)), const((1, SMALL_W)), const((1, DN_DK))]
    args = [proj, proj, proj, proj, proj_b, conv_w, _lane_vec(a_log, LANE_DN_A), _lane_vec(dt_bias, LANE_DN_A),
            norm_g.reshape(1, DN_DK)]
    if not zero_init:
        in_specs += list(_state_specs(layer, conv_state, state))
        args += [conv_state, state]
    st_spec, st_shape, n_prev = _stacked_state_out(prev_states, (b, DN_HEADS, DN_DK, DN_DK), in_specs, args)
    return pl.pallas_call(
        functools.partial(_dn_kernel, rows_in=rows_in, rows=rows, zero_init=zero_init, n_prev=n_prev),
        grid=(b, l // rows_in),
        in_specs=in_specs,
        out_specs=[pl.BlockSpec((None, rows_in, DN_WIDTH), lambda i, j: (i, j, 0)), st_spec],
        out_shape=[jax.ShapeDtypeStruct((b, l, DN_WIDTH), F32), st_shape],
        scratch_shapes=[pltpu.VMEM((8 + rows, 3 * DN_WIDTH), F32), pltpu.VMEM((rows, SMALL_W), F32)],
        compiler_params=_cparams(("parallel", "arbitrary")),
        name="deltanet",
    )(*args)


def _ssd_kernel(*refs, rows_in, rows, zero_init, n_prev):
    n_in = 9 + (0 if zero_init else 2) + (1 if n_prev else 0)
    ins, (o_ref, st_out, xbuf, smbuf) = refs[:n_in], refs[n_in:]
    x_ref, z_ref, sm_ref, cw_ref, cb_ref, alog_ref, dtb_ref, d_ref, gam_ref = ins[:9]
    cs_ref, h0_ref = (None, None) if zero_init else ins[9:11]
    st_ref = _carry_earlier_layers(st_out, ins[-1] if n_prev else None, n_prev)
    c = rows
    groups = range(SSM_GROUPS)
    heads = range(SSM_HEADS)

    @pl.when(pl.program_id(1) == 0)
    def _():
        st_ref[...] = jnp.zeros(st_ref.shape, F32) if zero_init else h0_ref[...]

    sm = _stage_rows(xbuf, smbuf, ((x_ref, 0),), sm_ref, cs_ref, rows_in, rows)
    dt_all = _softplus(sm + dtb_ref[...])
    if rows_in < rows:
        dt_all = dt_all * (_iota2((c, 1), 0) < rows_in).astype(F32)
    da_all = dt_all * (-jnp.exp(alog_ref[...]))
    ii = _iota2((c, c), 0)
    jj = _iota2((c, c), 1)
    tri = ii >= jj
    acum_all = _sel_dot(tri.astype(BF16), da_all)
    sel = (_iota2((SSM_HEADS, SMALL_W), 1) == _iota2((SSM_HEADS, SMALL_W), 0) + LANE_SSM_DT).astype(BF16)
    arow_all = _sel_dot_nt(sel, acum_all)
    dtrow_all = _sel_dot_nt(sel, dt_all)
    gn = SSM_GROUPS * SSM_N
    gw = SSM_HPG * SSM_P

    bm = [_conv_cols(xbuf, cw_ref, SSM_INNER + g * SSM_N, SSM_N, c, cb_ref) for g in groups]
    cm = [_conv_cols(xbuf, cw_ref, SSM_INNER + gn + g * SSM_N, SSM_N, c, cb_ref) for g in groups]
    xg = [_conv_cols(xbuf, cw_ref, g * gw, gw, c, cb_ref) for g in groups]
    cb = [_dot_nt(cm[g], bm[g]) for g in groups]
    grp = [h // SSM_HPG for h in heads]
    xh = [xg[grp[h]][:, (h % SSM_HPG) * SSM_P:(h % SSM_HPG + 1) * SSM_P] for h in heads]
    ac = [acum_all[:, LANE_SSM_DT + h:LANE_SSM_DT + h + 1] for h in heads]
    dc = [dt_all[:, LANE_SSM_DT + h:LANE_SSM_DT + h + 1] for h in heads]
    lmat = [jnp.where(tri, jnp.exp(jnp.where(tri, ac[h] - arow_all[h:h + 1, :], 0.0)), 0.0) for h in heads]
    h_prev = [st_ref[h] for h in heads]
    y = [_dot(cb[grp[h]] * lmat[h] * dtrow_all[h:h + 1, :], xh[h])
         + _dot_nt(cm[grp[h]], h_prev[h]) * jnp.exp(ac[h]) + d_ref[0:1, h:h + 1] * xh[h] for h in heads]
    a_last = [ac[h][c - 1:c, :] for h in heads]
    wdec = [dc[h] * jnp.exp(a_last[h] - ac[h]) for h in heads]
    h_new = [h_prev[h] * jnp.exp(a_last[h]) + _dot_tn(xh[h] * wdec[h], bm[grp[h]]) for h in heads]
    for h in heads:
        st_ref[h] = h_new[h]
    for g in groups:
        yg = jnp.concatenate(y[g * SSM_HPG:(g + 1) * SSM_HPG], axis=1)
        z = z_ref[:, g * gw:(g + 1) * gw].astype(F32)
        o_ref[:, g * gw:(g + 1) * gw] = _rms_scale(yg[0:rows_in, :] * _silu(z)) * gam_ref[0:1, g * gw:(g + 1) * gw]


def ssd(proj, proj_b, layer, conv_state, state, conv_w, conv_b, a_log, dt_bias, d_skip, norm_g, *, rows_in, rows,
        prev_states=None):
    b, l, _ = proj.shape
    zero_init = state is None
    blk = lambda w, off: pl.BlockSpec((None, rows_in, w), lambda i, j: (i, j, off // w))
    const = lambda shape: pl.BlockSpec(shape, lambda i, j: (0,) * len(shape))
    in_specs = [blk(SSM_CONV_DIM, OFF_XBC), blk(SSM_INNER, OFF_SSMZ), blk(SMALL_W, OFF_SMALL - OFF_B),
                const((CONV_W, SSM_CONV_DIM)), const((1, SSM_CONV_DIM)), const((1, SMALL_W)), const((1, SMALL_W)),
                const((1, SMALL_W)), const((1, SSM_INNER))]
    args = [proj, proj, proj_b, conv_w, conv_b.reshape(1, SSM_CONV_DIM), _lane_vec(a_log, LANE_SSM_DT),
            _lane_vec(dt_bias, LANE_SSM_DT), _lane_vec(d_skip, 0), norm_g.reshape(1, SSM_INNER)]
    if not zero_init:
        in_specs += list(_state_specs(layer, conv_state, state))
        args += [conv_state, state]
    st_spec, st_shape, n_prev = _stacked_state_out(prev_states, (b, SSM_HEADS, SSM_P, SSM_N), in_specs, args)
    return pl.pallas_call(
        functools.partial(_ssd_kernel, rows_in=rows_in, rows=rows, zero_init=zero_init, n_prev=n_prev),
        grid=(b, l // rows_in),
        in_specs=in_specs,
        out_specs=[pl.BlockSpec((None, rows_in, SSM_INNER), lambda i, j: (i, j, 0)), st_spec],
        out_shape=[jax.ShapeDtypeStruct((b, l, SSM_INNER), F32), st_shape],
        scratch_shapes=[pltpu.VMEM((8 + rows, SSM_CONV_DIM), F32), pltpu.VMEM((rows, SMALL_W), F32)],
        compiler_params=_cparams(("parallel", "arbitrary")),
        name="ssd",
    )(*args)


def _attend_blocks(q_ref, k_ref, v_ref, acc_ref, lse_ref, dil, blocks):
    span = HEAD
    scale = HEAD ** -0.5
    ii = _iota2((span, span), 0)
    jj = _iota2((span, span), 1)
    cur_ok = jj <= ii
    prev_ok = jj >= ii
    rows = lambda st: pl.ds(st, span) if dil == 1 else pl.ds(st, span, stride=dil)
    q = [q_ref[rows(st), :] for st, _, _ in blocks]
    s_c = [jnp.where(cur_ok, _dot_nt(q[i], k_ref[rows(st), :]) * scale, NEG_INF) for i, (st, _, _) in enumerate(blocks)]
    m = [jnp.max(s, axis=-1, keepdims=True) for s in s_c]
    s_p = []
    for i, (_, sp, ok) in enumerate(blocks):
        if sp is None:
            s_p.append(None)
            continue
        mask = prev_ok if ok is True else jnp.logical_and(prev_ok, ok)
        s_p.append(jnp.where(mask, _dot_nt(q[i], k_ref[rows(sp), :]) * scale, NEG_INF))
        m[i] = jnp.maximum(m[i], jnp.max(s_p[i], axis=-1, keepdims=True))
    for i, (st, sp, _) in enumerate(blocks):
        p_c = jnp.exp(s_c[i] - m[i])
        l = jnp.sum(p_c, axis=-1, keepdims=True)
        acc = _dot(p_c, v_ref[rows(st), :])
        if sp is not None:
            p_p = jnp.exp(s_p[i] - m[i])
            l = l + jnp.sum(p_p, axis=-1, keepdims=True)
            acc = acc + _dot(p_p, v_ref[rows(sp), :])
        acc_ref[rows(st), :] = acc / l
        lse_ref[rows(st), :] = jnp.broadcast_to(m[i] + jnp.log(l), (span, HEAD))


def _dil_prompt_kernel(*refs, seq):
    n_g = len(DIL_GROUPS)
    qs, ks, vs = refs[0:n_g], refs[n_g:2 * n_g], refs[2 * n_g:3 * n_g]
    o_ref, acc_s, lse_s = refs[3 * n_g:]
    u = ATTN_UNROLL
    for g, (_, dil) in enumerate(DIL_GROUPS):
        nb = seq // dil // HEAD
        attend = functools.partial(_attend_blocks, qs[g], ks[g], vs[g], acc_s.at[g], lse_s.at[g], dil)
        if dil == 1:
            assert nb % u == 0

            def body(it, carry, attend=attend):
                blocks = []
                for j in range(u):
                    st = pl.multiple_of((it * u + j) * HEAD, HEAD)
                    prev = pl.multiple_of(jnp.maximum(st - HEAD, 0), HEAD)
                    blocks.append((st, prev, (it > 0) if j == 0 else True))
                attend(blocks)
                return carry
            lax.fori_loop(0, nb // u, body, 0)
        elif nb > 1:
            def body(r, carry, attend=attend, dil=dil, nb=nb):
                attend([(r + n * dil * HEAD, None if n == 0 else r + (n - 1) * dil * HEAD, True) for n in range(nb)])
                return carry
            lax.fori_loop(0, dil, body, 0)
        else:
            assert dil % u == 0

            def body(it, carry, attend=attend):
                attend([(it * u + j, None, True) for j in range(u)])
                return carry
            lax.fori_loop(0, dil // u, body, 0)

    rows_per = 256

    def merge(it, carry):
        rows = pl.ds(pl.multiple_of(it * rows_per, rows_per), rows_per)
        lses = [lse_s[g, rows, :] for g in range(n_g)]
        top = functools.reduce(jnp.maximum, lses)
        wts = [jnp.exp(x - top) for x in lses]
        o_ref[rows, :] = sum(w * acc_s[g, rows, :] for g, w in enumerate(wts)) / sum(wts)
        return carry
    lax.fori_loop(0, seq // rows_per, merge, 0)


def dilated_prompt(proj):
    b, s, _ = proj.shape
    for _, dil in DIL_GROUPS:
        assert s % (dil * HEAD) == 0

    def src(off, g):
        base = off // HEAD + g * DIL_HPG
        return pl.BlockSpec((None, s, HEAD), lambda i, h: (i, 0, base + h))

    n_g = len(DIL_GROUPS)
    in_specs = [src(off - OFF_B, g) for off in (OFF_DILQ, OFF_DILK, OFF_DILV) for g in range(n_g)]
    return pl.pallas_call(
        functools.partial(_dil_prompt_kernel, seq=s),
        grid=(b, DIL_HPG),
        in_specs=in_specs,
        out_specs=pl.BlockSpec((None, s, HEAD), lambda i, h: (i, 0, h)),
        out_shape=jax.ShapeDtypeStruct((b, s, DIL_OUT), F32),
        scratch_shapes=[pltpu.VMEM((n_g, s, HEAD), F32), pltpu.VMEM((n_g, s, HEAD), F32)],
        compiler_params=_cparams(("parallel", "parallel")),
        name="dilated_prompt",
    )(*([proj] * (3 * n_g)))


def _dil_sample_kernel(new_ref, *rest, t_new):
    caches, o_ref = rest[:-1], rest[-1]
    nh = DIL_HPG
    nq = t_new * nh
    h_shift = int(math.log2(nh))
    n_g = len(DIL_GROUPS)
    scale = HEAD ** -0.5
    rounds = []
    for g, (_, dil) in enumerate(DIL_GROUPS):
        kc, vc = caches[2 * g], caches[2 * g + 1]
        if dil == 1:
            rounds.append((g, 0, nq, nh, dil, kc[...], vc[...]))
        else:
            rounds += [(g, 8 * r, 8, 8, dil, kc[:, 8 * r:8 * r + 8, :].reshape(kc.shape[0] * 8, HEAD),
                        vc[:, 8 * r:8 * r + 8, :].reshape(vc.shape[0] * 8, HEAD)) for r in range(nq // 8)]
    q = [new_ref[g][row0:row0 + nr, :] for g, row0, nr, _, _, _, _ in rounds]
    s_c, s_n = [], []
    for i, (g, row0, nr, width, dil, k2d, _) in enumerate(rounds):
        n_keys = k2d.shape[0]
        qi = _iota2((nr, n_keys), 0) + row0
        kj = _iota2((nr, n_keys), 1)
        same_slot = (kj & (width - 1)) == (qi & (width - 1))
        first = (qi >> h_shift) >> int(math.log2(dil))
        ok = jnp.logical_and(same_slot, (kj >> int(math.log2(width))) >= first)
        s_c.append(jnp.where(ok, _dot_nt(q[i], k2d) * scale, NEG_INF))
        qn = _iota2((nr, nq), 0) + row0
        kn = _iota2((nr, nq), 1)
        gap = (qn >> h_shift) - (kn >> h_shift)
        ok_n = jnp.logical_and((kn & (nh - 1)) == (qn & (nh - 1)),
                               jnp.logical_and(gap >= 0, (gap & (dil - 1)) == 0))
        s_n.append(jnp.where(ok_n, _dot_nt(q[i], new_ref[n_g + g]) * scale, NEG_INF))
    m = [jnp.maximum(jnp.max(a, axis=-1, keepdims=True), jnp.max(b, axis=-1, keepdims=True)) for a, b in zip(s_c, s_n)]
    p_c = [jnp.exp(a - mm) for a, mm in zip(s_c, m)]
    p_n = [jnp.exp(a - mm) for a, mm in zip(s_n, m)]
    l = [jnp.sum(a, axis=-1, keepdims=True) + jnp.sum(b, axis=-1, keepdims=True) for a, b in zip(p_c, p_n)]
    o = [(_dot(p_c[i], r[6]) + _dot(p_n[i], new_ref[2 * n_g + r[0]])) / l[i] for i, r in enumerate(rounds)]
    lse = [mm + jnp.log(ll) for mm, ll in zip(m, l)]
    outs, lses = [], []
    for g in range(n_g):
        mine = [i for i, r in enumerate(rounds) if r[0] == g]
        outs.append(o[mine[0]] if len(mine) == 1 else jnp.concatenate([o[i] for i in mine], axis=0))
        lses.append(lse[mine[0]] if len(mine) == 1 else jnp.concatenate([lse[i] for i in mine], axis=0))
    top = functools.reduce(jnp.maximum, lses)
    wts = [jnp.exp(x - top) for x in lses]
    o_ref[...] = sum(w * o for w, o in zip(wts, outs)) / sum(wts)


def dilated_sample(proj, layer, caches):
    b, t_new, _ = proj.shape
    nh = DIL_HPG
    assert nh == 4 and (t_new * nh) % 8 == 0
    views, specs = [], []
    for g, (window, dil) in enumerate(DIL_GROUPS):
        for c in caches[2 * g:2 * g + 2]:
            lb = c.shape[2]
            assert lb == window and lb % dil == 0 and (dil == 1 or t_new <= dil) and dil & (dil - 1) == 0
            if dil == 1:
                views.append(c.reshape(c.shape[0], b, lb * nh, HEAD))
                specs.append(pl.BlockSpec((None, None, lb * nh, HEAD), lambda i: (layer, i, 0, 0)))
            else:
                views.append(c.reshape(c.shape[0], b, lb // dil, dil * nh, HEAD))
                specs.append(pl.BlockSpec((None, None, lb // dil, t_new * nh, HEAD), lambda i: (layer, i, 0, 0, 0)))
    n_g = len(DIL_GROUPS)
    new = proj[:, :, OFF_DILQ - OFF_B:OFF_DILQ - OFF_B + 3 * DIL_WIDTH].reshape(b, t_new, 3, n_g, nh, HEAD)
    new = new.transpose(0, 2, 3, 1, 4, 5).reshape(b, 3 * n_g, t_new * nh, HEAD)
    out = pl.pallas_call(
        functools.partial(_dil_sample_kernel, t_new=t_new),
        grid=(b,),
        in_specs=[pl.BlockSpec((None, 3 * n_g, t_new * nh, HEAD), lambda i: (i, 0, 0, 0))] + specs,
        out_specs=pl.BlockSpec((None, t_new * nh, HEAD), lambda i: (i, 0, 0)),
        out_shape=jax.ShapeDtypeStruct((b, t_new * nh, HEAD), F32),
        compiler_params=_cparams(("parallel",)),
        name="dilated_sample",
    )(new, *views)
    return out.reshape(b, t_new, DIL_OUT)


def _merge_kernel(x_ref, odn_ref, odil_ref, ossm_ref, g_dn, g_dil, g_ssm, w_dn, w_dil, w_ssm, w_out, gam_ref, o_ref):
    merged = (_sigmoid(g_dn[...].astype(F32)) * _dot(odn_ref[...], w_dn[...])
              + _sigmoid(g_dil[...].astype(F32)) * _dot(odil_ref[...], w_dil[...])
              + _sigmoid(g_ssm[...].astype(F32)) * _dot(ossm_ref[...], w_ssm[...]))
    y = _dot(merged, w_out[...])
    o_ref[...] = x_ref[...] + _rms_scale(y) * gam_ref[...]


def branch_merge(x, o_dn, o_dil, o_ssm, proj, w_dn, w_dil, w_ssm, w_out, gam, *, tm):
    n = x.shape[0]
    row = lambda w: pl.BlockSpec((tm, w), lambda i: (i, 0))
    gate = lambda k: pl.BlockSpec((tm, D_MODEL), lambda i: (i, OFF_GATES // D_MODEL + k))
    const = lambda a: pl.BlockSpec(a.shape, lambda i: (0,) * a.ndim)
    gam = gam.reshape(1, D_MODEL)
    return pl.pallas_call(
        _merge_kernel,
        grid=(n // tm,),
        in_specs=[row(D_MODEL), row(DN_WIDTH), row(DIL_OUT), row(SSM_INNER), gate(0), gate(1), gate(2),
                  const(w_dn), const(w_dil), const(w_ssm), const(w_out), const(gam)],
        out_specs=row(D_MODEL),
        out_shape=jax.ShapeDtypeStruct((n, D_MODEL), F32),
        compiler_params=_cparams(("parallel",)),
        name="branch_merge",
    )(x, o_dn, o_dil, o_ssm, proj, proj, proj, w_dn, w_dil, w_ssm, w_out, gam)


def _mem_attn_kernel(q_ref, k_ref, v_ref, o_ref):
    scale = HEAD ** -0.5
    for h in range(MEM_HEADS):
        cols = slice(h * HEAD, (h + 1) * HEAD)
        s = _dot_nt(q_ref[:, cols], k_ref[:, cols]) * scale
        m = jnp.max(s, axis=-1, keepdims=True)
        p = jnp.exp(s - m)
        o_ref[:, cols] = _dot(p, v_ref[:, cols]) / jnp.sum(p, axis=-1, keepdims=True)


def mem_attention(q, kv, *, tq):
    b, l, _ = q.shape
    n_mem = kv.shape[1]
    return pl.pallas_call(
        _mem_attn_kernel,
        grid=(b, l // tq),
        in_specs=[pl.BlockSpec((None, tq, MEM_WIDTH), lambda i, j: (i, j, 0)),
                  pl.BlockSpec((None, n_mem, MEM_WIDTH), lambda i, j: (i, 0, 0)),
                  pl.BlockSpec((None, n_mem, MEM_WIDTH), lambda i, j: (i, 0, 1))],
        out_specs=pl.BlockSpec((None, tq, MEM_WIDTH), lambda i, j: (i, j, 0)),
        out_shape=jax.ShapeDtypeStruct((b, l, MEM_WIDTH), F32),
        compiler_params=_cparams(("parallel", "parallel")),
        name="mem_attention",
    )(q, kv, kv)


def _mem_attn_slot_kernel(q_ref, k_ref, v_ref, o_ref):
    scale = HEAD ** -0.5
    nq, n_keys = q_ref.shape[0], k_ref.shape[0]
    same_head = (_iota2((nq, n_keys), 1) & (MEM_HEADS - 1)) == (_iota2((nq, n_keys), 0) & (MEM_HEADS - 1))
    s = jnp.where(same_head, _dot_nt(q_ref[...], k_ref[...]) * scale, NEG_INF)
    m = jnp.max(s, axis=-1, keepdims=True)
    p = jnp.exp(s - m)
    o_ref[...] = _dot(p, v_ref[...]) / jnp.sum(p, axis=-1, keepdims=True)


def mem_attention_cached(q, layer, cache_k, cache_v):
    b, l, _ = q.shape
    n_mem = cache_k.shape[2]
    assert MEM_HEADS & (MEM_HEADS - 1) == 0 and (l * MEM_HEADS) % 8 == 0
    rows = lambda a: a.reshape(a.shape[0], b, n_mem * MEM_HEADS, HEAD)
    kv_spec = pl.BlockSpec((None, None, n_mem * MEM_HEADS, HEAD), lambda i: (layer, i, 0, 0))
    out = pl.pallas_call(
        _mem_attn_slot_kernel,
        grid=(b,),
        in_specs=[pl.BlockSpec((None, l * MEM_HEADS, HEAD), lambda i: (i, 0, 0)), kv_spec, kv_spec],
        out_specs=pl.BlockSpec((None, l * MEM_HEADS, HEAD), lambda i: (i, 0, 0)),
        out_shape=jax.ShapeDtypeStruct((b, l * MEM_HEADS, HEAD), F32),
        compiler_params=_cparams(("parallel",)),
        name="mem_attention_cached",
    )(q.reshape(b, l * MEM_HEADS, HEAD), rows(cache_k), rows(cache_v))
    return out.reshape(b, l, MEM_WIDTH)


def _matmul_norm_res_kernel(x_ref, a_ref, w_ref, gam_ref, o_ref):
    y = _dot(a_ref[...], w_ref[...])
    o_ref[...] = x_ref[...] + _rms_scale(y) * gam_ref[...]


def matmul_norm_residual(x, a, w, gam, *, tm):
    n, k = a.shape
    return pl.pallas_call(
        _matmul_norm_res_kernel,
        grid=(n // tm,),
        in_specs=[pl.BlockSpec((tm, D_MODEL), lambda i: (i, 0)), pl.BlockSpec((tm, k), lambda i: (i, 0)),
                  pl.BlockSpec((k, D_MODEL), lambda i: (0, 0)), pl.BlockSpec((1, D_MODEL), lambda i: (0, 0))],
        out_specs=pl.BlockSpec((tm, D_MODEL), lambda i: (i, 0)),
        out_shape=jax.ShapeDtypeStruct((n, D_MODEL), F32),
        compiler_params=_cparams(("parallel",)),
        name="matmul_norm_residual",
    )(x, a, w, gam.reshape(1, D_MODEL))


def _mlp_kernel(x_ref, gpre_ref, w1_ref, w2_ref, gpost_ref, o_ref, h_ref, acc_ref):
    k = pl.program_id(1)

    @pl.when(k == 0)
    def _():
        h_ref[...] = (_rms_scale(x_ref[...]) * gpre_ref[...]).astype(BF16)
        acc_ref[...] = jnp.zeros(acc_ref.shape, F32)

    f = jnp.maximum(jnp.dot(h_ref[...], w1_ref[...], preferred_element_type=F32), 0.0)
    acc_ref[...] += _dot(f * f, w2_ref[...])

    @pl.when(k == pl.num_programs(1) - 1)
    def _():
        o_ref[...] = x_ref[...] + _rms_scale(acc_ref[...]) * gpost_ref[...]


def mlp(x, gpre, w1, w2, gpost, *, tm, tk):
    n = x.shape[0]
    return pl.pallas_call(
        _mlp_kernel,
        grid=(n // tm, D_FF // tk),
        in_specs=[pl.BlockSpec((tm, D_MODEL), lambda i, k: (i, 0)), pl.BlockSpec((1, D_MODEL), lambda i, k: (0, 0)),
                  pl.BlockSpec((D_MODEL, tk), lambda i, k: (0, k)), pl.BlockSpec((tk, D_MODEL), lambda i, k: (k, 0)),
                  pl.BlockSpec((1, D_MODEL), lambda i, k: (0, 0))],
        out_specs=pl.BlockSpec((tm, D_MODEL), lambda i, k: (i, 0)),
        out_shape=jax.ShapeDtypeStruct((n, D_MODEL), F32),
        scratch_shapes=[pltpu.VMEM((tm, D_MODEL), BF16), pltpu.VMEM((tm, D_MODEL), F32)],
        compiler_params=_cparams(("parallel", "arbitrary")),
        name="mlp",
    )(x, gpre.reshape(1, D_MODEL), w1, w2, gpost.reshape(1, D_MODEL))


def reorder_w_in(w_in):
    idx = []
    acc = 0
    for s in IN_SIZES:
        idx.append((acc, acc + s))
        acc += s
    dn_qkv, dn_z, dn_b, dn_a, dil_qkv, ssm_z, ssm_xbc, ssm_dt, gates = [w_in[:, a:b] for a, b in idx]
    small = jnp.concatenate([dn_b, dn_a, ssm_dt], axis=1)
    pad = jnp.zeros((w_in.shape[0], NP - OFF_SMALL - small.shape[1]), w_in.dtype)
    w_a = jnp.concatenate([dn_qkv, dn_z, ssm_xbc, ssm_z, gates], axis=1).astype(BF16)
    w_b = jnp.concatenate([dil_qkv, small, pad], axis=1).astype(BF16)
    return w_a, w_b


def trunk_layer(x, layer, rope, mem, states, p, cfg, carry=(None, None)):
    b, l, _ = x.shape
    n = b * l
    tm = cfg["tm"]
    xf = x.reshape(n, D_MODEL)
    w_a, w_b = p["w_in"]
    proj = norm_matmul(xf, p["norm_mix_pre"], w_a, tm=cfg["tm_in"], tn=1024, out_dtype=cfg["proj_dtype"])
    proj_b = norm_matmul(xf, p["norm_mix_pre"], w_b, tm=cfg["tm_in"], tn=1024,
                         rope=rope + ((OFF_DILQ - OFF_B) // 1024, (OFF_DILV - OFF_B) // 1024))
    proj3 = proj.reshape(b, l, OFF_B)
    proj_b3 = proj_b.reshape(b, l, NP - OFF_B)
    dn_conv, dn_state, ssm_conv, ssm_state = states[:4] if states is not None else (None,) * 4

    o_dn, dn_state_new = deltanet(proj3, proj_b3, layer, dn_conv, dn_state, p["dn_conv_w"], p["dn_a_log"],
                                  p["dn_dt_bias"], p["dn_norm"], rows_in=cfg["rows_in"], rows=cfg["rows"],
                                  prev_states=carry[0])
    o_ssm, ssm_state_new = ssd(proj3, proj_b3, layer, ssm_conv, ssm_state, p["ssm_conv_w"], p["ssm_conv_b"],
                               p["ssm_a_log"], p["ssm_dt_bias"], p["ssm_d"], p["ssm_norm"],
                               rows_in=cfg["rows_in_ssd"], rows=cfg["rows_ssd"], prev_states=carry[1])
    win_new = []
    o_dil = dilated_prompt(proj_b3) if states is None else dilated_sample(proj_b3, layer, states[4:])
    for gi, (window, _) in enumerate(DIL_GROUPS):
        keep = min(window, l) if states is None else l
        for off in (OFF_DILK, OFF_DILV):
            c0 = off - OFF_B + gi * DIL_OUT
            win_new.append(proj_b3[:, l - keep:, c0:c0 + DIL_OUT])

    xf = branch_merge(xf, o_dn.reshape(n, DN_WIDTH), o_dil.reshape(n, DIL_OUT), o_ssm.reshape(n, SSM_INNER), proj,
                      p["w_br_dn"], p["w_br_dil"], p["w_br_ssm"], p["w_out"], p["norm_mix_post"], tm=cfg["tm_merge"])

    qm = norm_matmul(xf, p["norm_mem_pre"], p["w_mq"], tm=tm, tn=MEM_WIDTH).reshape(b, l, MEM_WIDTH)
    om = mem_attention(qm, mem, tq=cfg["tq"]) if states is None else mem_attention_cached(qm, layer, *mem)
    xf = matmul_norm_residual(xf, om.reshape(n, MEM_WIDTH), p["w_mo"], p["norm_mem_post"], tm=tm)

    xf = mlp(xf, p["norm_ffn_pre"], p["w_ff1"], p["w_ff2"], p["norm_ffn_post"], tm=tm, tk=1024)

    def conv_tail(buf, off, width):
        new = proj3[:, :, off:off + width].astype(F32)
        if l < CONV_W - 1:
            prev = jnp.zeros((b, CONV_W - 1, width), F32) if buf is None else buf[layer]
            new = jnp.concatenate([prev, new], axis=1)
        return new[:, new.shape[1] - (CONV_W - 1):]

    new_states = (conv_tail(dn_conv, OFF_DNQKV, 3 * DN_WIDTH), dn_state_new,
                  conv_tail(ssm_conv, OFF_XBC, SSM_CONV_DIM), ssm_state_new) + tuple(win_new)
    return xf.reshape(b, l, D_MODEL), new_states


def kernel(x_prompt, x_sample, state_dn_conv, state_dn, state_ssm_conv, state_ssm, cache_win1_k, cache_win1_v, cache_win2_k, cache_win2_v, cache_win3_k, cache_win3_v, cache_mem_k, cache_mem_v, mem_prompt, norm_mix_pre, w_in, dn_conv_w, dn_a_log, dn_dt_bias, dn_norm, ssm_conv_w, ssm_conv_b, ssm_a_log, ssm_dt_bias, ssm_d, ssm_norm, w_br_dn, w_br_dil, w_br_ssm, w_out, norm_mix_post, norm_mem_pre, norm_mem_kv, w_mq, w_mkv, w_mo, norm_mem_post, norm_ffn_pre, w_ff1, w_ff2, norm_ffn_post):
    n_p, s = x_prompt.shape[:2]
    n_s, t = x_sample.shape[:2]
    n_mem = mem_prompt.shape[1]
    depth = w_in.shape[0]
    tm_p, tm_s = 1024, n_s * t
    rope_p = rope_tables(jnp.arange(s, dtype=F32))
    rope_s = rope_tables(jnp.tile(PAST_LEN + jnp.arange(t, dtype=F32), tm_s // t))
    cfg_p = dict(tm=tm_p, tm_in=2 * tm_p, tm_merge=256, rows_in=64, rows=64, rows_in_ssd=128, rows_ssd=128, tq=512,
                 proj_dtype=BF16)
    cfg_s = dict(tm=tm_s, tm_in=tm_s, tm_merge=tm_s, rows_in=t, rows=8, rows_in_ssd=t, rows_ssd=8, tq=t,
                 proj_dtype=F32)
    states_s = (state_dn_conv, state_dn, state_ssm_conv, state_ssm, cache_win1_k, cache_win1_v, cache_win2_k,
                cache_win2_v, cache_win3_k, cache_win3_v)
    xp, xs = x_prompt, x_sample
    carry_s = (None, None)
    new_p = [[] for _ in range(12)]
    new_s = [[] for _ in range(10)]
    for l in range(depth):
        bf = lambda a: a[l].astype(BF16)
        prm = dict(norm_mix_pre=norm_mix_pre[l], w_in=reorder_w_in(w_in[l]), dn_conv_w=dn_conv_w[l],
                   dn_a_log=dn_a_log[l], dn_dt_bias=dn_dt_bias[l], dn_norm=dn_norm[l], ssm_conv_w=ssm_conv_w[l],
                   ssm_conv_b=ssm_conv_b[l], ssm_a_log=ssm_a_log[l], ssm_dt_bias=ssm_dt_bias[l], ssm_d=ssm_d[l],
                   ssm_norm=ssm_norm[l], w_br_dn=bf(w_br_dn), w_br_dil=bf(w_br_dil), w_br_ssm=bf(w_br_ssm),
                   w_out=bf(w_out), norm_mix_post=norm_mix_post[l], norm_mem_pre=norm_mem_pre[l], w_mq=bf(w_mq),
                   w_mo=bf(w_mo), norm_mem_post=norm_mem_post[l], norm_ffn_pre=norm_ffn_pre[l], w_ff1=bf(w_ff1),
                   w_ff2=bf(w_ff2), norm_ffn_post=norm_ffn_post[l])
        mkv = norm_matmul(mem_prompt.reshape(n_p * n_mem, D_MODEL), norm_mem_kv[l], bf(w_mkv),
                          tm=min(1024, n_p * n_mem), tn=1024)
        mkv = mkv.reshape(n_p, n_mem, 2 * MEM_WIDTH)
        mk = mkv[:, :, :MEM_WIDTH].reshape(n_p, n_mem, MEM_HEADS, HEAD)
        mv = mkv[:, :, MEM_WIDTH:].reshape(n_p, n_mem, MEM_HEADS, HEAD)
        xp, st_p = trunk_layer(xp, l, rope_p, mkv, None, prm, cfg_p)
        for i, a in enumerate(st_p + (mk, mv)):
            new_p[i].append(a)
        xs, st_s = trunk_layer(xs, l, rope_s, (cache_mem_k, cache_mem_v), states_s, prm, cfg_s, carry_s)
        carry_s = tuple(a if l > 0 else a[None] for a in (st_s[1], st_s[3]))
        for i, a in enumerate(st_s):
            new_s[i].append(a)

    def finish(per_layer, i):
        a = jnp.stack(per_layer)
        if 4 <= i < 4 + 2 * len(DIL_GROUPS):
            a = a.reshape(a.shape[:3] + (DIL_HPG, HEAD))
        return a

    outs_p = [finish(a, i) for i, a in enumerate(new_p)]
    outs_s = [carry_s[(1, 3).index(i)] if i in (1, 3) else finish(a, i) for i, a in enumerate(new_s)]
    return (xp, xs, *outs_p, *outs_s)
```

```python
import functools
import math

import jax
import jax.numpy as jnp
from jax import lax
from jax.experimental import pallas as pl
from jax.experimental.pallas import tpu as pltpu

F32 = jnp.float32
BF16 = jnp.bfloat16

D_MODEL = 1024
CONV_W = 4
DN_HEADS = 8
DN_DK = 128
DN_WIDTH = DN_HEADS * DN_DK
DIL_GROUPS = ((128, 1), (512, 4), (2048, 16))
DIL_HPG = 4
HEAD = 128
DIL_HEADS = DIL_HPG * len(DIL_GROUPS)
DIL_WIDTH = DIL_HEADS * HEAD
DIL_OUT = DIL_HPG * HEAD
ROT_DIM = HEAD // 4
ROPE_THETA = 500000.0
PAST_LEN = 2048
SSM_HEADS = 16
SSM_P = 64
SSM_N = 128
SSM_GROUPS = 4
SSM_HPG = SSM_HEADS // SSM_GROUPS
SSM_INNER = SSM_HEADS * SSM_P
SSM_CONV_DIM = SSM_INNER + 2 * SSM_GROUPS * SSM_N
MEM_HEADS = 4
MEM_WIDTH = MEM_HEADS * HEAD
D_FF = 4 * D_MODEL
EPS = 1e-6
NEG_INF = -1e30
IN_SIZES = (3 * DN_WIDTH, DN_WIDTH, DN_HEADS, DN_HEADS, 3 * DIL_WIDTH, SSM_INNER, SSM_CONV_DIM, SSM_HEADS,
            3 * D_MODEL)

OFF_DNQKV = 0
OFF_DNZ = OFF_DNQKV + 3 * DN_WIDTH
OFF_XBC = OFF_DNZ + DN_WIDTH
OFF_SSMZ = OFF_XBC + SSM_CONV_DIM
OFF_GATES = OFF_SSMZ + SSM_INNER
OFF_DILQ = OFF_GATES + 3 * D_MODEL
OFF_DILK = OFF_DILQ + DIL_WIDTH
OFF_DILV = OFF_DILK + DIL_WIDTH
OFF_SMALL = OFF_DILV + DIL_WIDTH
SMALL_W = 128
NP = 15 * 1024
OFF_B = OFF_DILQ
LANE_DN_B = 0
LANE_DN_A = DN_HEADS
LANE_SSM_DT = 2 * DN_HEADS

SOLVE_BASE = 16
ATTN_UNROLL = 4

VMEM_LIMIT = 56 * 1024 * 1024


def _cparams(sem):
    return pltpu.CompilerParams(dimension_semantics=sem, vmem_limit_bytes=VMEM_LIMIT)


def _dot(a, b):
    return jnp.dot(a.astype(BF16), b.astype(BF16), preferred_element_type=F32)


def _dot_nt(a, b):
    return lax.dot_general(a.astype(BF16), b.astype(BF16), (((1,), (1,)), ((), ())), preferred_element_type=F32)


def _dot_tn(a, b):
    return lax.dot_general(a.astype(BF16), b.astype(BF16), (((0,), (0,)), ((), ())), preferred_element_type=F32)


def _split3(x):
    hi = x.astype(BF16)
    r = x - hi.astype(F32)
    mid = r.astype(BF16)
    lo = (r - mid.astype(F32)).astype(BF16)
    return hi, mid, lo


def _sel_dot(sel, x):
    return sum(jnp.dot(sel, p, preferred_element_type=F32) for p in _split3(x))


def _sel_dot_nt(sel, x):
    return sum(lax.dot_general(sel, p, (((1,), (1,)), ((), ())), preferred_element_type=F32) for p in _split3(x))


def _rms_scale(x):
    return x * lax.rsqrt(jnp.mean(x * x, axis=-1, keepdims=True) + EPS)


def _softplus(x):
    return jnp.maximum(x, 0.0) + jnp.log1p(jnp.exp(-jnp.abs(x)))


def _sigmoid(x):
    return 1.0 / (1.0 + jnp.exp(-x))


def _silu(x):
    return x * _sigmoid(x)


def _iota2(shape, axis):
    return lax.broadcasted_iota(jnp.int32, shape, axis)


def _norm_matmul_kernel(*refs, rope_tiles, tn):
    if rope_tiles is None:
        x_ref, g_ref, w_ref, o_ref, h_ref = refs
    else:
        x_ref, g_ref, w_ref, cos_ref, sa_ref, sb_ref, o_ref, h_ref = refs
    j = pl.program_id(1)

    @pl.when(j == 0)
    def _():
        h_ref[...] = (_rms_scale(x_ref[...]) * g_ref[...]).astype(BF16)

    y = jnp.dot(h_ref[...], w_ref[...], preferred_element_type=F32)
    if rope_tiles is None:
        o_ref[...] = y.astype(o_ref.dtype)
        return
    lo, hi = rope_tiles
    is_rope = jnp.logical_and(j >= lo, j < hi)

    @pl.when(is_rope)
    def _():
        cos, sa, sb = cos_ref[...], sa_ref[...], sb_ref[...]
        for c in range(tn // HEAD):
            t = y[:, c * HEAD:(c + 1) * HEAD]
            o_ref[:, c * HEAD:(c + 1) * HEAD] = (
                t * cos + pltpu.roll(t, HEAD - ROT_DIM // 2, 1) * sa + pltpu.roll(t, ROT_DIM // 2, 1) * sb)

    @pl.when(jnp.logical_not(is_rope))
    def _():
        o_ref[...] = y


def norm_matmul(x, g, w, *, tm, tn, rope=None, out_dtype=F32):
    n, k = x.shape
    m = w.shape[1]
    assert n % tm == 0 and m % tn == 0
    in_specs = [pl.BlockSpec((tm, k), lambda i, j: (i, 0)),
                pl.BlockSpec((1, k), lambda i, j: (0, 0)),
                pl.BlockSpec((k, tn), lambda i, j: (0, j))]
    args = [x, g.reshape(1, k), w]
    rope_tiles = None
    if rope is not None:
        cos, sa, sb, lo, hi = rope
        period = cos.shape[0] // tm
        assert cos.shape[0] % tm == 0
        tab = pl.BlockSpec((tm, HEAD), lambda i, j: (i % period, 0))
        in_specs += [tab, tab, tab]
        args += [cos, sa, sb]
        rope_tiles = (lo, hi)
    return pl.pallas_call(
        functools.partial(_norm_matmul_kernel, rope_tiles=rope_tiles, tn=tn),
        grid=(n // tm, m // tn),
        in_specs=in_specs,
        out_specs=pl.BlockSpec((tm, tn), lambda i, j: (i, j)),
        out_shape=jax.ShapeDtypeStruct((n, m), out_dtype),
        scratch_shapes=[pltpu.VMEM((tm, k), BF16)],
        compiler_params=_cparams(("parallel", "arbitrary")),
        name="norm_matmul",
    )(*args)


def rope_tables(pos):
    half = ROT_DIM // 2
    inv = jnp.power(ROPE_THETA, -jnp.arange(half, dtype=F32) * 2.0 / ROT_DIM)
    ang = pos[:, None] * inv[None, :]
    cos, sin = jnp.cos(ang), jnp.sin(ang)
    n = pos.shape[0]
    ones = jnp.ones((n, HEAD - ROT_DIM), F32)
    zeros = jnp.zeros((n, HEAD - ROT_DIM), F32)
    z16 = jnp.zeros((n, half), F32)
    return (jnp.concatenate([cos, cos, ones], axis=1),
            jnp.concatenate([-sin, z16, zeros], axis=1),
            jnp.concatenate([z16, sin, zeros], axis=1))


def _conv_cols(xbuf, cw_ref, c0, width, rows, bias_ref=None):
    acc = xbuf[5:5 + rows, c0:c0 + width] * cw_ref[0:1, c0:c0 + width]
    for j in range(1, CONV_W):
        acc = acc + xbuf[5 + j:5 + j + rows, c0:c0 + width] * cw_ref[j:j + 1, c0:c0 + width]
    if bias_ref is not None:
        acc = acc + bias_ref[0:1, c0:c0 + width]
    return _silu(acc)


def _stage_rows(xbuf, smbuf, parts, sm_ref, cs_ref, rows_in, rows):
    @pl.when(pl.program_id(1) == 0)
    def _():
        xbuf[5:8, :] = jnp.zeros((CONV_W - 1, xbuf.shape[1]), F32) if cs_ref is None else cs_ref[...]

    @pl.when(pl.program_id(1) > 0)
    def _():
        xbuf[5:8, :] = xbuf[5 + rows_in:8 + rows_in, :]

    for ref, c0 in parts:
        xbuf[8:8 + rows_in, c0:c0 + ref.shape[1]] = ref[...].astype(F32)
    if rows_in < rows:
        xbuf[8 + rows_in:8 + rows, :] = jnp.zeros((rows - rows_in, xbuf.shape[1]), F32)
        smbuf[...] = jnp.zeros(smbuf.shape, F32)
        smbuf[0:rows_in, :] = sm_ref[...]
        return smbuf[...]
    return sm_ref[...]


def _lane_vec(v, lane0):
    return jnp.zeros((1, SMALL_W), F32).at[0, lane0:lane0 + v.shape[0]].set(v)


def _state_specs(layer, conv_state, state):
    cs = pl.BlockSpec((None, None) + conv_state.shape[2:], lambda i, j: (layer, i, 0, 0))
    st = pl.BlockSpec((None, None) + state.shape[2:], lambda i, j: (layer, i, 0, 0, 0))
    return cs, st


def _carry_earlier_layers(st_out, prev_ref, n_prev):
    if not n_prev:
        return st_out

    @pl.when(pl.program_id(1) == 0)
    def _():
        st_out[0:n_prev] = prev_ref[...]

    return st_out.at[n_prev]


def _dn_kernel(*refs, rows_in, rows, zero_init, n_prev):
    n_in = 9 + (0 if zero_init else 2) + (1 if n_prev else 0)
    ins, (o_ref, st_out, xbuf, smbuf) = refs[:n_in], refs[n_in:]
    q_ref, k_ref, v_ref, z_ref, sm_ref, cw_ref, alog_ref, dtb_ref, gam_ref = ins[:9]
    cs_ref, s0_ref = (None, None) if zero_init else ins[9:11]
    st_ref = _carry_earlier_layers(st_out, ins[-1] if n_prev else None, n_prev)
    c = rows
    heads = range(DN_HEADS)

    @pl.when(pl.program_id(1) == 0)
    def _():
        st_ref[...] = jnp.zeros(st_ref.shape, F32) if zero_init else s0_ref[...]

    sm = _stage_rows(xbuf, smbuf, ((q_ref, 0), (k_ref, DN_WIDTH), (v_ref, 2 * DN_WIDTH)), sm_ref, cs_ref,
                     rows_in, rows)
    beta_all = _sigmoid(sm)
    g_all = -jnp.exp(alog_ref[...]) * _softplus(sm + dtb_ref[...])
    if rows_in < rows:
        live = (_iota2((c, 1), 0) < rows_in).astype(F32)
        beta_all = beta_all * live
        g_all = g_all * live
    ii = _iota2((c, c), 0)
    jj = _iota2((c, c), 1)
    tri = ii >= jj
    strict = ii > jj
    eye = (ii == jj).astype(F32)
    gcum_all = _sel_dot(tri.astype(BF16), g_all)
    sel = (_iota2((DN_HEADS, SMALL_W), 1) == _iota2((DN_HEADS, SMALL_W), 0) + LANE_DN_A).astype(BF16)
    grow_all = _sel_dot_nt(sel, gcum_all)
    base = min(c, SOLVE_BASE)
    n_sq = int(math.log2(base)) - 1
    n_lvl = int(math.log2(c // base))
    assert 2 ** (n_sq + 1) == base and base * 2 ** n_lvl == c
    same_blk = [(ii >> (n_sq + 1 + lvl)) == (jj >> (n_sq + 1 + lvl)) for lvl in range(n_lvl + 1)]

    q = [_conv_cols(xbuf, cw_ref, h * DN_DK, DN_DK, c) for h in heads]
    k = [_conv_cols(xbuf, cw_ref, DN_WIDTH + h * DN_DK, DN_DK, c) for h in heads]
    v = [_conv_cols(xbuf, cw_ref, 2 * DN_WIDTH + h * DN_DK, DN_DK, c) for h in heads]
    q = [x * lax.rsqrt(jnp.sum(x * x, axis=-1, keepdims=True) + EPS) * (DN_DK ** -0.5) for x in q]
    k = [x * lax.rsqrt(jnp.sum(x * x, axis=-1, keepdims=True) + EPS) for x in k]
    gc = [gcum_all[:, LANE_DN_A + h:LANE_DN_A + h + 1] for h in heads]
    gr = [grow_all[h:h + 1, :] for h in heads]
    beta = [beta_all[:, LANE_DN_B + h:LANE_DN_B + h + 1] for h in heads]
    decay = [jnp.where(tri, jnp.exp(jnp.where(tri, gc[h] - gr[h], 0.0)), 0.0) for h in heads]
    eg = [jnp.exp(gc[h]) for h in heads]
    kb = [k[h] * beta[h] for h in heads]
    a_mat = [_dot_nt(kb[h], k[h]) * jnp.where(strict, decay[h], 0.0) for h in heads]
    qk = [_dot_nt(q[h], k[h]) * decay[h] for h in heads]
    x = [-jnp.where(same_blk[0], a_mat[h], 0.0) for h in heads]
    t_inv = [eye + x[h] for h in heads]
    for _ in range(n_sq):
        x = [_dot(x[h], x[h]) for h in heads]
        t_inv = [t_inv[h] + _dot(t_inv[h], x[h]) for h in heads]
    for lvl in range(1, len(same_blk)):
        ring = jnp.logical_and(same_blk[lvl], jnp.logical_not(same_blk[lvl - 1]))
        left = [_dot(t_inv[h], jnp.where(ring, a_mat[h], 0.0)) for h in heads]
        t_inv = [t_inv[h] - _dot(left[h], t_inv[h]) for h in heads]
    sol = [_dot(t_inv[h], jnp.concatenate([v[h] * beta[h], kb[h] * eg[h]], axis=1)) for h in heads]
    s_prev = [st_ref[h] for h in heads]
    v_new = [sol[h][:, :DN_DK] - _dot(sol[h][:, DN_DK:], s_prev[h]) for h in heads]
    o = [_dot(q[h] * eg[h], s_prev[h]) + _dot(qk[h], v_new[h]) for h in heads]
    g_last = [gc[h][c - 1:c, :] for h in heads]
    k_dec = [k[h] * jnp.exp(g_last[h] - gc[h]) for h in heads]
    s_new = [s_prev[h] * jnp.exp(g_last[h]) + _dot_tn(k_dec[h], v_new[h]) for h in heads]
    for h in heads:
        st_ref[h] = s_new[h]
        z = z_ref[:, h * DN_DK:(h + 1) * DN_DK].astype(F32)
        o_ref[:, h * DN_DK:(h + 1) * DN_DK] = _rms_scale(o[h][0:rows_in, :]) * gam_ref[...] * _silu(z)


def _stacked_state_out(prev_states, shape, in_specs, args):
    nd = len(shape)
    if prev_states is None:
        return (pl.BlockSpec((None,) + shape[1:], lambda i, j: (i,) + (0,) * (nd - 1)),
                jax.ShapeDtypeStruct(shape, F32), 0)
    n_prev = prev_states.shape[0]
    in_specs.append(pl.BlockSpec((n_prev, None) + shape[1:], lambda i, j: (0, i) + (0,) * (nd - 1)))
    args.append(prev_states)
    return (pl.BlockSpec((n_prev + 1, None) + shape[1:], lambda i, j: (0, i) + (0,) * (nd - 1)),
            jax.ShapeDtypeStruct((n_prev + 1,) + shape, F32), n_prev)


def deltanet(proj, proj_b, layer, conv_state, state, conv_w, a_log, dt_bias, norm_g, *, rows_in, rows,
             prev_states=None):
    b, l, _ = proj.shape
    assert l % rows_in == 0 and rows_in <= rows
    zero_init = state is None
    blk = lambda w, off: pl.BlockSpec((None, rows_in, w), lambda i, j: (i, j, off // w))
    const = lambda shape: pl.BlockSpec(shape, lambda i, j: (0,) * len(shape))
    in_specs = [blk(DN_WIDTH, OFF_DNQKV), blk(DN_WIDTH, OFF_DNQKV + DN_WIDTH), blk(DN_WIDTH, OFF_DNQKV + 2 * DN_WIDTH),
                blk(DN_WIDTH, OFF_DNZ), blk(SMALL_W, OFF_SMALL - OFF_B),
                const((CONV_W, 3 * DN_WIDTH)), const((1, SMALL_W)), const((1, SMALL_W)), const((1, DN_DK))]
    args = [proj, proj, proj, proj, proj_b, conv_w, _lane_vec(a_log, LANE_DN_A), _lane_vec(dt_bias, LANE_DN_A),
            norm_g.reshape(1, DN_DK)]
    if not zero_init:
        in_specs += list(_state_specs(layer, conv_state, state))
        args += [conv_state, state]
    st_spec, st_shape, n_prev = _stacked_state_out(prev_states, (b, DN_HEADS, DN_DK, DN_DK), in_specs, args)
    return pl.pallas_call(
        functools.partial(_dn_kernel, rows_in=rows_in, rows=rows, zero_init=zero_init, n_prev=n_prev),
        grid=(b, l // rows_in),
        in_specs=in_specs,
        out_specs=[pl.BlockSpec((None, rows_in, DN_WIDTH), lambda i, j: (i, j, 0)), st_spec],
        out_shape=[jax.ShapeDtypeStruct((b, l, DN_WIDTH), F32), st_shape],
        scratch_shapes=[pltpu.VMEM((8 + rows, 3 * DN_WIDTH), F32), pltpu.VMEM((rows, SMALL_W), F32)],
        compiler_params=_cparams(("parallel", "arbitrary")),
        name="deltanet",
    )(*args)


def _ssd_kernel(*refs, rows_in, rows, zero_init, n_prev):
    n_in = 9 + (0 if zero_init else 2) + (1 if n_prev else 0)
    ins, (o_ref, st_out, xbuf, smbuf) = refs[:n_in], refs[n_in:]
    x_ref, z_ref, sm_ref, cw_ref, cb_ref, alog_ref, dtb_ref, d_ref, gam_ref = ins[:9]
    cs_ref, h0_ref = (None, None) if zero_init else ins[9:11]
    st_ref = _carry_earlier_layers(st_out, ins[-1] if n_prev else None, n_prev)
    c = rows
    groups = range(SSM_GROUPS)
    heads = range(SSM_HEADS)

    @pl.when(pl.program_id(1) == 0)
    def _():
        st_ref[...] = jnp.zeros(st_ref.shape, F32) if zero_init else h0_ref[...]

    sm = _stage_rows(xbuf, smbuf, ((x_ref, 0),), sm_ref, cs_ref, rows_in, rows)
    dt_all = _softplus(sm + dtb_ref[...])
    if rows_in < rows:
        dt_all = dt_all * (_iota2((c, 1), 0) < rows_in).astype(F32)
    da_all = dt_all * (-jnp.exp(alog_ref[...]))
    ii = _iota2((c, c), 0)
    jj = _iota2((c, c), 1)
    tri = ii >= jj
    acum_all = _sel_dot(tri.astype(BF16), da_all)
    sel = (_iota2((SSM_HEADS, SMALL_W), 1) == _iota2((SSM_HEADS, SMALL_W), 0) + LANE_SSM_DT).astype(BF16)
    arow_all = _sel_dot_nt(sel, acum_all)
    dtrow_all = _sel_dot_nt(sel, dt_all)
    gn = SSM_GROUPS * SSM_N
    gw = SSM_HPG * SSM_P

    bm = [_conv_cols(xbuf, cw_ref, SSM_INNER + g * SSM_N, SSM_N, c, cb_ref) for g in groups]
    cm = [_conv_cols(xbuf, cw_ref, SSM_INNER + gn + g * SSM_N, SSM_N, c, cb_ref) for g in groups]
    xg = [_conv_cols(xbuf, cw_ref, g * gw, gw, c, cb_ref) for g in groups]
    cb = [_dot_nt(cm[g], bm[g]) for g in groups]
    grp = [h // SSM_HPG for h in heads]
    xh = [xg[grp[h]][:, (h % SSM_HPG) * SSM_P:(h % SSM_HPG + 1) * SSM_P] for h in heads]
    ac = [acum_all[:, LANE_SSM_DT + h:LANE_SSM_DT + h + 1] for h in heads]
    dc = [dt_all[:, LANE_SSM_DT + h:LANE_SSM_DT + h + 1] for h in heads]
    lmat = [jnp.where(tri, jnp.exp(jnp.where(tri, ac[h] - arow_all[h:h + 1, :], 0.0)), 0.0) for h in heads]
    h_prev = [st_ref[h] for h in heads]
    y = [_dot(cb[grp[h]] * lmat[h] * dtrow_all[h:h + 1, :], xh[h])
         + _dot_nt(cm[grp[h]], h_prev[h]) * jnp.exp(ac[h]) + d_ref[0:1, h:h + 1] * xh[h] for h in heads]
    a_last = [ac[h][c - 1:c, :] for h in heads]
    wdec = [dc[h] * jnp.exp(a_last[h] - ac[h]) for h in heads]
    h_new = [h_prev[h] * jnp.exp(a_last[h]) + _dot_tn(xh[h] * wdec[h], bm[grp[h]]) for h in heads]
    for h in heads:
        st_ref[h] = h_new[h]
    for g in groups:
        yg = jnp.concatenate(y[g * SSM_HPG:(g + 1) * SSM_HPG], axis=1)
        z = z_ref[:, g * gw:(g + 1) * gw].astype(F32)
        o_ref[:, g * gw:(g + 1) * gw] = _rms_scale(yg[0:rows_in, :] * _silu(z)) * gam_ref[0:1, g * gw:(g + 1) * gw]


def ssd(proj, proj_b, layer, conv_state, state, conv_w, conv_b, a_log, dt_bias, d_skip, norm_g, *, rows_in, rows,
        prev_states=None):
    b, l, _ = proj.shape
    zero_init = state is None
    blk = lambda w, off: pl.BlockSpec((None, rows_in, w), lambda i, j: (i, j, off // w))
    const = lambda shape: pl.BlockSpec(shape, lambda i, j: (0,) * len(shape))
    in_specs = [blk(SSM_CONV_DIM, OFF_XBC), blk(SSM_INNER, OFF_SSMZ), blk(SMALL_W, OFF_SMALL - OFF_B),
                const((CONV_W, SSM_CONV_DIM)), const((1, SSM_CONV_DIM)), const((1, SMALL_W)), const((1, SMALL_W)),
                const((1, SMALL_W)), const((1, SSM_INNER))]
    args = [proj, proj, proj_b, conv_w, conv_b.reshape(1, SSM_CONV_DIM), _lane_vec(a_log, LANE_SSM_DT),
            _lane_vec(dt_bias, LANE_SSM_DT), _lane_vec(d_skip, 0), norm_g.reshape(1, SSM_INNER)]
    if not zero_init:
        in_specs += list(_state_specs(layer, conv_state, state))
        args += [conv_state, state]
    st_spec, st_shape, n_prev = _stacked_state_out(prev_states, (b, SSM_HEADS, SSM_P, SSM_N), in_specs, args)
    return pl.pallas_call(
        functools.partial(_ssd_kernel, rows_in=rows_in, rows=rows, zero_init=zero_init, n_prev=n_prev),
        grid=(b, l // rows_in),
        in_specs=in_specs,
        out_specs=[pl.BlockSpec((None, rows_in, SSM_INNER), lambda i, j: (i, j, 0)), st_spec],
        out_shape=[jax.ShapeDtypeStruct((b, l, SSM_INNER), F32), st_shape],
        scratch_shapes=[pltpu.VMEM((8 + rows, SSM_CONV_DIM), F32), pltpu.VMEM((rows, SMALL_W), F32)],
        compiler_params=_cparams(("parallel", "arbitrary")),
        name="ssd",
    )(*args)


def _attend_blocks(q_ref, k_ref, v_ref, acc_ref, lse_ref, dil, blocks):
    span = HEAD
    scale = HEAD ** -0.5
    ii = _iota2((span, span), 0)
    jj = _iota2((span, span), 1)
    cur_ok = jj <= ii
    prev_ok = jj >= ii
    rows = lambda st: pl.ds(st, span) if dil == 1 else pl.ds(st, span, stride=dil)
    q = [q_ref[rows(st), :] for st, _, _ in blocks]
    s_c = [jnp.where(cur_ok, _dot_nt(q[i], k_ref[rows(st), :]) * scale, NEG_INF) for i, (st, _, _) in enumerate(blocks)]
    m = [jnp.max(s, axis=-1, keepdims=True) for s in s_c]
    s_p = []
    for i, (_, sp, ok) in enumerate(blocks):
        if sp is None:
            s_p.append(None)
            continue
        mask = prev_ok if ok is True else jnp.logical_and(prev_ok, ok)
        s_p.append(jnp.where(mask, _dot_nt(q[i], k_ref[rows(sp), :]) * scale, NEG_INF))
        m[i] = jnp.maximum(m[i], jnp.max(s_p[i], axis=-1, keepdims=True))
    for i, (st, sp, _) in enumerate(blocks):
        p_c = jnp.exp(s_c[i] - m[i])
        l = jnp.sum(p_c, axis=-1, keepdims=True)
        acc = _dot(p_c, v_ref[rows(st), :])
        if sp is not None:
            p_p = jnp.exp(s_p[i] - m[i])
            l = l + jnp.sum(p_p, axis=-1, keepdims=True)
            acc = acc + _dot(p_p, v_ref[rows(sp), :])
        acc_ref[rows(st), :] = acc / l
        lse_ref[rows(st), :] = jnp.broadcast_to(m[i] + jnp.log(l), (span, HEAD))


def _dil_prompt_kernel(*refs, seq):
    n_g = len(DIL_GROUPS)
    qs, ks, vs = refs[0:n_g], refs[n_g:2 * n_g], refs[2 * n_g:3 * n_g]
    o_ref, acc_s, lse_s = refs[3 * n_g:]
    u = ATTN_UNROLL
    for g, (_, dil) in enumerate(DIL_GROUPS):
        nb = seq // dil // HEAD
        attend = functools.partial(_attend_blocks, qs[g], ks[g], vs[g], acc_s.at[g], lse_s.at[g], dil)
        if dil == 1:
            assert nb % u == 0

            def body(it, carry, attend=attend):
                blocks = []
                for j in range(u):
                    st = pl.multiple_of((it * u + j) * HEAD, HEAD)
                    prev = pl.multiple_of(jnp.maximum(st - HEAD, 0), HEAD)
                    blocks.append((st, prev, (it > 0) if j == 0 else True))
                attend(blocks)
                return carry
            lax.fori_loop(0, nb // u, body, 0)
        elif nb > 1:
            def body(r, carry, attend=attend, dil=dil, nb=nb):
                attend([(r + n * dil * HEAD, None if n == 0 else r + (n - 1) * dil * HEAD, True) for n in range(nb)])
                return carry
            lax.fori_loop(0, dil, body, 0)
        else:
            assert dil % u == 0

            def body(it, carry, attend=attend):
                attend([(it * u + j, None, True) for j in range(u)])
                return carry
            lax.fori_loop(0, dil // u, body, 0)

    rows_per = 256

    def merge(it, carry):
        rows = pl.ds(pl.multiple_of(it * rows_per, rows_per), rows_per)
        lses = [lse_s[g, rows, :] for g in range(n_g)]
        top = functools.reduce(jnp.maximum, lses)
        wts = [jnp.exp(x - top) for x in lses]
        o_ref[rows, :] = sum(w * acc_s[g, rows, :] for g, w in enumerate(wts)) / sum(wts)
        return carry
    lax.fori_loop(0, seq // rows_per, merge, 0)


def dilated_prompt(proj):
    b, s, _ = proj.shape
    for _, dil in DIL_GROUPS:
        assert s % (dil * HEAD) == 0

    def src(off, g):
        base = off // HEAD + g * DIL_HPG
        return pl.BlockSpec((None, s, HEAD), lambda i, h: (i, 0, base + h))

    n_g = len(DIL_GROUPS)
    in_specs = [src(off - OFF_B, g) for off in (OFF_DILQ, OFF_DILK, OFF_DILV) for g in range(n_g)]
    return pl.pallas_call(
        functools.partial(_dil_prompt_kernel, seq=s),
        grid=(b, DIL_HPG),
        in_specs=in_specs,
        out_specs=pl.BlockSpec((None, s, HEAD), lambda i, h: (i, 0, h)),
        out_shape=jax.ShapeDtypeStruct((b, s, DIL_OUT), F32),
        scratch_shapes=[pltpu.VMEM((n_g, s, HEAD), F32), pltpu.VMEM((n_g, s, HEAD), F32)],
        compiler_params=_cparams(("parallel", "parallel")),
        name="dilated_prompt",
    )(*([proj] * (3 * n_g)))


def _dil_sample_kernel(new_ref, *rest, t_new):
    caches, o_ref = rest[:-1], rest[-1]
    nh = DIL_HPG
    nq = t_new * nh
    h_shift = int(math.log2(nh))
    n_g = len(DIL_GROUPS)
    scale = HEAD ** -0.5
    rounds = []
    for g, (_, dil) in enumerate(DIL_GROUPS):
        kc, vc = caches[2 * g], caches[2 * g + 1]
        if dil == 1:
            rounds.append((g, 0, nq, nh, dil, kc[...], vc[...]))
        else:
            rounds += [(g, 8 * r, 8, 8, dil, kc[:, 8 * r:8 * r + 8, :].reshape(kc.shape[0] * 8, HEAD),
                        vc[:, 8 * r:8 * r + 8, :].reshape(vc.shape[0] * 8, HEAD)) for r in range(nq // 8)]
    q = [new_ref[g][row0:row0 + nr, :] for g, row0, nr, _, _, _, _ in rounds]
    s_c, s_n = [], []
    for i, (g, row0, nr, width, dil, k2d, _) in enumerate(rounds):
        n_keys = k2d.shape[0]
        qi = _iota2((nr, n_keys), 0) + row0
        kj = _iota2((nr, n_keys), 1)
        same_slot = (kj & (width - 1)) == (qi & (width - 1))
        first = (qi >> h_shift) >> int(math.log2(dil))
        ok = jnp.logical_and(same_slot, (kj >> int(math.log2(width))) >= first)
        s_c.append(jnp.where(ok, _dot_nt(q[i], k2d) * scale, NEG_INF))
        qn = _iota2((nr, nq), 0) + row0
        kn = _iota2((nr, nq), 1)
        gap = (qn >> h_shift) - (kn >> h_shift)
        ok_n = jnp.logical_and((kn & (nh - 1)) == (qn & (nh - 1)),
                               jnp.logical_and(gap >= 0, (gap & (dil - 1)) == 0))
        s_n.append(jnp.where(ok_n, _dot_nt(q[i], new_ref[n_g + g]) * scale, NEG_INF))
    m = [jnp.maximum(jnp.max(a, axis=-1, keepdims=True), jnp.max(b, axis=-1, keepdims=True)) for a, b in zip(s_c, s_n)]
    p_c = [jnp.exp(a - mm) for a, mm in zip(s_c, m)]
    p_n = [jnp.exp(a - mm) for a, mm in zip(s_n, m)]
    l = [jnp.sum(a, axis=-1, keepdims=True) + jnp.sum(b, axis=-1, keepdims=True) for a, b in zip(p_c, p_n)]
    o = [(_dot(p_c[i], r[6]) + _dot(p_n[i], new_ref[2 * n_g + r[0]])) / l[i] for i, r in enumerate(rounds)]
    lse = [mm + jnp.log(ll) for mm, ll in zip(m, l)]
    outs, lses = [], []
    for g in range(n_g):
        mine = [i for i, r in enumerate(rounds) if r[0] == g]
        outs.append(o[mine[0]] if len(mine) == 1 else jnp.concatenate([o[i] for i in mine], axis=0))
        lses.append(lse[mine[0]] if len(mine) == 1 else jnp.concatenate([lse[i] for i in mine], axis=0))
    top = functools.reduce(jnp.maximum, lses)
    wts = [jnp.exp(x - top) for x in lses]
    o_ref[...] = sum(w * o for w, o in zip(wts, outs)) / sum(wts)


def dilated_sample(proj, layer, caches):
    b, t_new, _ = proj.shape
    nh = DIL_HPG
    assert nh == 4 and (t_new * nh) % 8 == 0
    views, specs = [], []
    for g, (window, dil) in enumerate(DIL_GROUPS):
        for c in caches[2 * g:2 * g + 2]:
            lb = c.shape[2]
            assert lb == window and lb % dil == 0 and (dil == 1 or t_new <= dil) and dil & (dil - 1) == 0
            if dil == 1:
                views.append(c.reshape(c.shape[0], b, lb * nh, HEAD))
                specs.append(pl.BlockSpec((None, None, lb * nh, HEAD), lambda i: (layer, i, 0, 0)))
            else:
                views.append(c.reshape(c.shape[0], b, lb // dil, dil * nh, HEAD))
                specs.append(pl.BlockSpec((None, None, lb // dil, t_new * nh, HEAD), lambda i: (layer, i, 0, 0, 0)))
    n_g = len(DIL_GROUPS)
    new = proj[:, :, OFF_DILQ - OFF_B:OFF_DILQ - OFF_B + 3 * DIL_WIDTH].reshape(b, t_new, 3, n_g, nh, HEAD)
    new = new.transpose(0, 2, 3, 1, 4, 5).reshape(b, 3 * n_g, t_new * nh, HEAD)
    out = pl.pallas_call(
        functools.partial(_dil_sample_kernel, t_new=t_new),
        grid=(b,),
        in_specs=[pl.BlockSpec((None, 3 * n_g, t_new * nh, HEAD), lambda i: (i, 0, 0, 0))] + specs,
        out_specs=pl.BlockSpec((None, t_new * nh, HEAD), lambda i: (i, 0, 0)),
        out_shape=jax.ShapeDtypeStruct((b, t_new * nh, HEAD), F32),
        compiler_params=_cparams(("parallel",)),
        name="dilated_sample",
    )(new, *views)
    return out.reshape(b, t_new, DIL_OUT)


def _merge_kernel(x_ref, odn_ref, odil_ref, ossm_ref, g_dn, g_dil, g_ssm, w_dn, w_dil, w_ssm, w_out, gam_ref, o_ref):
    merged = (_sigmoid(g_dn[...].astype(F32)) * _dot(odn_ref[...], w_dn[...])
              + _sigmoid(g_dil[...].astype(F32)) * _dot(odil_ref[...], w_dil[...])
              + _sigmoid(g_ssm[...].astype(F32)) * _dot(ossm_ref[...], w_ssm[...]))
    y = _dot(merged, w_out[...])
    o_ref[...] = x_ref[...] + _rms_scale(y) * gam_ref[...]


def branch_merge(x, o_dn, o_dil, o_ssm, proj, w_dn, w_dil, w_ssm, w_out, gam, *, tm):
    n = x.shape[0]
    row = lambda w: pl.BlockSpec((tm, w), lambda i: (i, 0))
    gate = lambda k: pl.BlockSpec((tm, D_MODEL), lambda i: (i, OFF_GATES // D_MODEL + k))
    const = lambda a: pl.BlockSpec(a.shape, lambda i: (0,) * a.ndim)
    gam = gam.reshape(1, D_MODEL)
    return pl.pallas_call(
        _merge_kernel,
        grid=(n // tm,),
        in_specs=[row(D_MODEL), row(DN_WIDTH), row(DIL_OUT), row(SSM_INNER), gate(0), gate(1), gate(2),
                  const(w_dn), const(w_dil), const(w_ssm), const(w_out), const(gam)],
        out_specs=row(D_MODEL),
        out_shape=jax.ShapeDtypeStruct((n, D_MODEL), F32),
        compiler_params=_cparams(("parallel",)),
        name="branch_merge",
    )(x, o_dn, o_dil, o_ssm, proj, proj, proj, w_dn, w_dil, w_ssm, w_out, gam)


def _mem_attn_kernel(q_ref, k_ref, v_ref, o_ref):
    scale = HEAD ** -0.5
    for h in range(MEM_HEADS):
        cols = slice(h * HEAD, (h + 1) * HEAD)
        s = _dot_nt(q_ref[:, cols], k_ref[:, cols]) * scale
        m = jnp.max(s, axis=-1, keepdims=True)
        p = jnp.exp(s - m)
        o_ref[:, cols] = _dot(p, v_ref[:, cols]) / jnp.sum(p, axis=-1, keepdims=True)


def mem_attention(q, kv, *, tq):
    b, l, _ = q.shape
    n_mem = kv.shape[1]
    return pl.pallas_call(
        _mem_attn_kernel,
        grid=(b, l // tq),
        in_specs=[pl.BlockSpec((None, tq, MEM_WIDTH), lambda i, j: (i, j, 0)),
                  pl.BlockSpec((None, n_mem, MEM_WIDTH), lambda i, j: (i, 0, 0)),
                  pl.BlockSpec((None, n_mem, MEM_WIDTH), lambda i, j: (i, 0, 1))],
        out_specs=pl.BlockSpec((None, tq, MEM_WIDTH), lambda i, j: (i, j, 0)),
        out_shape=jax.ShapeDtypeStruct((b, l, MEM_WIDTH), F32),
        compiler_params=_cparams(("parallel", "parallel")),
        name="mem_attention",
    )(q, kv, kv)


def _mem_attn_slot_kernel(q_ref, k_ref, v_ref, o_ref):
    scale = HEAD ** -0.5
    nq, n_keys = q_ref.shape[0], k_ref.shape[0]
    same_head = (_iota2((nq, n_keys), 1) & (MEM_HEADS - 1)) == (_iota2((nq, n_keys), 0) & (MEM_HEADS - 1))
    s = jnp.where(same_head, _dot_nt(q_ref[...], k_ref[...]) * scale, NEG_INF)
    m = jnp.max(s, axis=-1, keepdims=True)
    p = jnp.exp(s - m)
    o_ref[...] = _dot(p, v_ref[...]) / jnp.sum(p, axis=-1, keepdims=True)


def mem_attention_cached(q, layer, cache_k, cache_v):
    b, l, _ = q.shape
    n_mem = cache_k.shape[2]
    assert MEM_HEADS & (MEM_HEADS - 1) == 0 and (l * MEM_HEADS) % 8 == 0
    rows = lambda a: a.reshape(a.shape[0], b, n_mem * MEM_HEADS, HEAD)
    kv_spec = pl.BlockSpec((None, None, n_mem * MEM_HEADS, HEAD), lambda i: (layer, i, 0, 0))
    out = pl.pallas_call(
        _mem_attn_slot_kernel,
        grid=(b,),
        in_specs=[pl.BlockSpec((None, l * MEM_HEADS, HEAD), lambda i: (i, 0, 0)), kv_spec, kv_spec],
        out_specs=pl.BlockSpec((None, l * MEM_HEADS, HEAD), lambda i: (i, 0, 0)),
        out_shape=jax.ShapeDtypeStruct((b, l * MEM_HEADS, HEAD), F32),
        compiler_params=_cparams(("parallel",)),
        name="mem_attention_cached",
    )(q.reshape(b, l * MEM_HEADS, HEAD), rows(cache_k), rows(cache_v))
    return out.reshape(b, l, MEM_WIDTH)


def _matmul_norm_res_kernel(x_ref, a_ref, w_ref, gam_ref, o_ref):
    y = _dot(a_ref[...], w_ref[...])
    o_ref[...] = x_ref[...] + _rms_scale(y) * gam_ref[...]


def matmul_norm_residual(x, a, w, gam, *, tm):
    n, k = a.shape
    return pl.pallas_call(
        _matmul_norm_res_kernel,
        grid=(n // tm,),
        in_specs=[pl.BlockSpec((tm, D_MODEL), lambda i: (i, 0)), pl.BlockSpec((tm, k), lambda i: (i, 0)),
                  pl.BlockSpec((k, D_MODEL), lambda i: (0, 0)), pl.BlockSpec((1, D_MODEL), lambda i: (0, 0))],
        out_specs=pl.BlockSpec((tm, D_MODEL), lambda i: (i, 0)),
        out_shape=jax.ShapeDtypeStruct((n, D_MODEL), F32),
        compiler_params=_cparams(("parallel",)),
        name="matmul_norm_residual",
    )(x, a, w, gam.reshape(1, D_MODEL))


def _mlp_kernel(x_ref, gpre_ref, w1_ref, w2_ref, gpost_ref, o_ref, h_ref, acc_ref):
    k = pl.program_id(1)

    @pl.when(k == 0)
    def _():
        h_ref[...] = (_rms_scale(x_ref[...]) * gpre_ref[...]).astype(BF16)
        acc_ref[...] = jnp.zeros(acc_ref.shape, F32)

    f = jnp.maximum(jnp.dot(h_ref[...], w1_ref[...], preferred_element_type=F32), 0.0)
    acc_ref[...] += _dot(f * f, w2_ref[...])

    @pl.when(k == pl.num_programs(1) - 1)
    def _():
        o_ref[...] = x_ref[...] + _rms_scale(acc_ref[...]) * gpost_ref[...]


def mlp(x, gpre, w1, w2, gpost, *, tm, tk):
    n = x.shape[0]
    return pl.pallas_call(
        _mlp_kernel,
        grid=(n // tm, D_FF // tk),
        in_specs=[pl.BlockSpec((tm, D_MODEL), lambda i, k: (i, 0)), pl.BlockSpec((1, D_MODEL), lambda i, k: (0, 0)),
                  pl.BlockSpec((D_MODEL, tk), lambda i, k: (0, k)), pl.BlockSpec((tk, D_MODEL), lambda i, k: (k, 0)),
                  pl.BlockSpec((1, D_MODEL), lambda i, k: (0, 0))],
        out_specs=pl.BlockSpec((tm, D_MODEL), lambda i, k: (i, 0)),
        out_shape=jax.ShapeDtypeStruct((n, D_MODEL), F32),
        scratch_shapes=[pltpu.VMEM((tm, D_MODEL), BF16), pltpu.VMEM((tm, D_MODEL), F32)],
        compiler_params=_cparams(("parallel", "arbitrary")),
        name="mlp",
    )(x, gpre.reshape(1, D_MODEL), w1, w2, gpost.reshape(1, D_MODEL))


def reorder_w_in(w_in):
    idx = []
    acc = 0
    for s in IN_SIZES:
        idx.append((acc, acc + s))
        acc += s
    dn_qkv, dn_z, dn_b, dn_a, dil_qkv, ssm_z, ssm_xbc, ssm_dt, gates = [w_in[:, a:b] for a, b in idx]
    small = jnp.concatenate([dn_b, dn_a, ssm_dt], axis=1)
    pad = jnp.zeros((w_in.shape[0], NP - OFF_SMALL - small.shape[1]), w_in.dtype)
    w_a = jnp.concatenate([dn_qkv, dn_z, ssm_xbc, ssm_z, gates], axis=1).astype(BF16)
    w_b = jnp.concatenate([dil_qkv, small, pad], axis=1).astype(BF16)
    return w_a, w_b


def trunk_layer(x, layer, rope, mem, states, p, cfg, carry=(None, None)):
    b, l, _ = x.shape
    n = b * l
    tm = cfg["tm"]
    xf = x.reshape(n, D_MODEL)
    w_a, w_b = p["w_in"]
    proj = norm_matmul(xf, p["norm_mix_pre"], w_a, tm=cfg["tm_in"], tn=1024, out_dtype=cfg["proj_dtype"])
    proj_b = norm_matmul(xf, p["norm_mix_pre"], w_b, tm=cfg["tm_in"], tn=1024,
                         rope=rope + ((OFF_DILQ - OFF_B) // 1024, (OFF_DILV - OFF_B) // 1024))
    proj3 = proj.reshape(b, l, OFF_B)
    proj_b3 = proj_b.reshape(b, l, NP - OFF_B)
    dn_conv, dn_state, ssm_conv, ssm_state = states[:4] if states is not None else (None,) * 4

    o_dn, dn_state_new = deltanet(proj3, proj_b3, layer, dn_conv, dn_state, p["dn_conv_w"], p["dn_a_log"],
                                  p["dn_dt_bias"], p["dn_norm"], rows_in=cfg["rows_in"], rows=cfg["rows"],
                                  prev_states=carry[0])
    o_ssm, ssm_state_new = ssd(proj3, proj_b3, layer, ssm_conv, ssm_state, p["ssm_conv_w"], p["ssm_conv_b"],
                               p["ssm_a_log"], p["ssm_dt_bias"], p["ssm_d"], p["ssm_norm"],
                               rows_in=cfg["rows_in_ssd"], rows=cfg["rows_ssd"], prev_states=carry[1])
    win_new = []
    o_dil = dilated_prompt(proj_b3) if states is None else dilated_sample(proj_b3, layer, states[4:])
    for gi, (window, _) in enumerate(DIL_GROUPS):
        keep = min(window, l) if states is None else l
        for off in (OFF_DILK, OFF_DILV):
            c0 = off - OFF_B + gi * DIL_OUT
            win_new.append(proj_b3[:, l - keep:, c0:c0 + DIL_OUT])

    xf = branch_merge(xf, o_dn.reshape(n, DN_WIDTH), o_dil.reshape(n, DIL_OUT), o_ssm.reshape(n, SSM_INNER), proj,
                      p["w_br_dn"], p["w_br_dil"], p["w_br_ssm"], p["w_out"], p["norm_mix_post"], tm=cfg["tm_merge"])

    qm = norm_matmul(xf, p["norm_mem_pre"], p["w_mq"], tm=tm, tn=MEM_WIDTH).reshape(b, l, MEM_WIDTH)
    om = mem_attention(qm, mem, tq=cfg["tq"]) if states is None else mem_attention_cached(qm, layer, *mem)
    xf = matmul_norm_residual(xf, om.reshape(n, MEM_WIDTH), p["w_mo"], p["norm_mem_post"], tm=tm)

    xf = mlp(xf, p["norm_ffn_pre"], p["w_ff1"], p["w_ff2"], p["norm_ffn_post"], tm=tm, tk=1024)

    def conv_tail(buf, off, width):
        new = proj3[:, :, off:off + width].astype(F32)
        if l < CONV_W - 1:
            prev = jnp.zeros((b, CONV_W - 1, width), F32) if buf is None else buf[layer]
            new = jnp.concatenate([prev, new], axis=1)
        return new[:, new.shape[1] - (CONV_W - 1):]

    new_states = (conv_tail(dn_conv, OFF_DNQKV, 3 * DN_WIDTH), dn_state_new,
                  conv_tail(ssm_conv, OFF_XBC, SSM_CONV_DIM), ssm_state_new) + tuple(win_new)
    return xf.reshape(b, l, D_MODEL), new_states


def kernel(x_prompt, x_sample, state_dn_conv, state_dn, state_ssm_conv, state_ssm, cache_win1_k, cache_win1_v, cache_win2_k, cache_win2_v, cache_win3_k, cache_win3_v, cache_mem_k, cache_mem_v, mem_prompt, norm_mix_pre, w_in, dn_conv_w, dn_a_log, dn_dt_bias, dn_norm, ssm_conv_w, ssm_conv_b, ssm_a_log, ssm_dt_bias, ssm_d, ssm_norm, w_br_dn, w_br_dil, w_br_ssm, w_out, norm_mix_post, norm_mem_pre, norm_mem_kv, w_mq, w_mkv, w_mo, norm_mem_post, norm_ffn_pre, w_ff1, w_ff2, norm_ffn_post):
    n_p, s = x_prompt.shape[:2]
    n_s, t = x_sample.shape[:2]
    n_mem = mem_prompt.shape[1]
    depth = w_in.shape[0]
    tm_p, tm_s = 1024, n_s * t
    rope_p = rope_tables(jnp.arange(s, dtype=F32))
    rope_s = rope_tables(jnp.tile(PAST_LEN + jnp.arange(t, dtype=F32), tm_s // t))
    cfg_p = dict(tm=tm_p, tm_in=2 * tm_p, tm_merge=256, rows_in=64, rows=64, rows_in_ssd=256, rows_ssd=256, tq=512,
                 proj_dtype=BF16)
    cfg_s = dict(tm=tm_s, tm_in=tm_s, tm_merge=tm_s, rows_in=t, rows=8, rows_in_ssd=t, rows_ssd=8, tq=t,
                 proj_dtype=F32)
    states_s = (state_dn_conv, state_dn, state_ssm_conv, state_ssm, cache_win1_k, cache_win1_v, cache_win2_k,
                cache_win2_v, cache_win3_k, cache_win3_v)
    xp, xs = x_prompt, x_sample
    carry_s = (None, None)
    new_p = [[] for _ in range(12)]
    new_s = [[] for _ in range(10)]
    for l in range(depth):
        bf = lambda a: a[l].astype(BF16)
        prm = dict(norm_mix_pre=norm_mix_pre[l], w_in=reorder_w_in(w_in[l]), dn_conv_w=dn_conv_w[l],
                   dn_a_log=dn_a_log[l], dn_dt_bias=dn_dt_bias[l], dn_norm=dn_norm[l], ssm_conv_w=ssm_conv_w[l],
                   ssm_conv_b=ssm_conv_b[l], ssm_a_log=ssm_a_log[l], ssm_dt_bias=ssm_dt_bias[l], ssm_d=ssm_d[l],
                   ssm_norm=ssm_norm[l], w_br_dn=bf(w_br_dn), w_br_dil=bf(w_br_dil), w_br_ssm=bf(w_br_ssm),
                   w_out=bf(w_out), norm_mix_post=norm_mix_post[l], norm_mem_pre=norm_mem_pre[l], w_mq=bf(w_mq),
                   w_mo=bf(w_mo), norm_mem_post=norm_mem_post[l], norm_ffn_pre=norm_ffn_pre[l], w_ff1=bf(w_ff1),
                   w_ff2=bf(w_ff2), norm_ffn_post=norm_ffn_post[l])
        mkv = norm_matmul(mem_prompt.reshape(n_p * n_mem, D_MODEL), norm_mem_kv[l], bf(w_mkv),
                          tm=min(1024, n_p * n_mem), tn=1024)
        mkv = mkv.reshape(n_p, n_mem, 2 * MEM_WIDTH)
        mk = mkv[:, :, :MEM_WIDTH].reshape(n_p, n_mem, MEM_HEADS, HEAD)
        mv = mkv[:, :, MEM_WIDTH:].reshape(n_p, n_mem, MEM_HEADS, HEAD)
        xp, st_p = trunk_layer(xp, l, rope_p, mkv, None, prm, cfg_p)
        for i, a in enumerate(st_p + (mk, mv)):
            new_p[i].append(a)
        xs, st_s = trunk_layer(xs, l, rope_s, (cache_mem_k, cache_mem_v), states_s, prm, cfg_s, carry_s)
        carry_s = tuple(a if l > 0 else a[None] for a in (st_s[1], st_s[3]))
        for i, a in enumerate(st_s):
            new_s[i].append(a)

    def finish(per_layer, i):
        a = jnp.stack(per_layer)
        if 4 <= i < 4 + 2 * len(DIL_GROUPS):
            a = a.reshape(a.shape[:3] + (DIL_HPG, HEAD))
        return a

    outs_p = [finish(a, i) for i, a in enumerate(new_p)]
    outs_s = [carry_s[(1, 3).index(i)] if i in (1, 3) else finish(a, i) for i, a in enumerate(new_s)]
    return (xp, xs, *outs_p, *outs_s)
```
